```python
import math
import jax, jax.numpy as jnp
from jax import lax
import numpy as np

D_MODEL = 1024
BATCH = 2
SEQ = 8192
DEPTH = 2
DEC_BATCH = 32
DEC_SEQ = 1
PAST_LEN = 8192
PAGE_SIZE = 128

H_SB = 8
HD_SB = 64
H_FOX = 8
HD_FOX = 64
H_ML = 4
DK_ML = 256
DV_ML = 256
MLSTM_CHUNK = 64
Q_BLOCK = 128
D_FF = ((8 * D_MODEL // 3 + 127) // 128) * 128
N_ATT_LAYERS = (DEPTH + 1) // 2
N_ML_LAYERS = DEPTH // 2
ATT_IN = 3 * H_SB * HD_SB + 3 * H_FOX * HD_FOX + H_FOX
ML_IN = 2 * H_ML * DK_ML + 2 * H_ML * DV_ML + 2 * H_ML
SB_SCALE = HD_SB ** -0.5
FOX_SCALE = HD_FOX ** -0.5
NORM_EPS = 1e-6
ML_FORGET_BIAS = 3.0

kernel_name = 'hybrid_sb_fox_mlstm_decode_step'


def rms_norm(x, g):
    xf = x.astype(jnp.float32)
    y = xf * lax.rsqrt(jnp.mean(xf * xf, axis=-1, keepdims=True) + NORM_EPS) * g.astype(jnp.float32)
    return y.astype(x.dtype)


def swiglu(h, w_in, w_out):
    gate, up = jnp.split(h @ w_in, 2, axis=-1)
    return (jax.nn.silu(gate) * up) @ w_out


def gather_pages(pool, page_table):
    rows = pool[page_table]
    return rows.reshape(rows.shape[0], rows.shape[1] * rows.shape[2], *rows.shape[3:])


def to_query_blocks(a, nb, qb):
    return jnp.swapaxes(a.reshape(a.shape[0], nb, qb, *a.shape[2:]), 0, 1)


def from_query_blocks(a):
    a = jnp.swapaxes(a, 0, 1)
    return a.reshape(a.shape[0], a.shape[1] * a.shape[2], *a.shape[3:])


def stick_breaking_attention(q, k, v, q_pos, k_pos):
    T = q.shape[1]
    qb = math.gcd(T, Q_BLOCK)
    nb = T // qb

    def block(args):
        q_blk, qp = args
        z = jnp.einsum('bqhd,bshd->bhqs', q_blk, k).astype(jnp.float32) * SB_SCALE
        mask = k_pos[None, :] < qp[:, None]
        log_1m = jnp.where(mask, jax.nn.log_sigmoid(-z), 0.0)
        later = lax.cumsum(log_1m, axis=3, reverse=True) - log_1m
        w = jnp.where(mask, jnp.exp(jax.nn.log_sigmoid(z) + later), 0.0)
        return jnp.einsum('bhqs,bshd->bqhd', w.astype(v.dtype), v)

    out = lax.map(block, (to_query_blocks(q, nb, qb), q_pos.reshape(nb, qb)))
    return from_query_blocks(out)


def forgetting_attention(q, k, v, c_q, c_k, q_pos, k_pos):
    T = q.shape[1]
    qb = math.gcd(T, Q_BLOCK)
    nb = T // qb
    ck = jnp.transpose(c_k, (0, 2, 1))[:, :, None, :]

    def block(args):
        q_blk, cq_blk, qp = args
        s = (jnp.einsum('bqhd,bshd->bhqs', q_blk, k).astype(jnp.float32) * FOX_SCALE
             + jnp.transpose(cq_blk, (0, 2, 1))[..., None] - ck)
        mask = k_pos[None, :] <= qp[:, None]
        p = jax.nn.softmax(jnp.where(mask, s, -jnp.inf), axis=-1)
        return jnp.einsum('bhqs,bshd->bqhd', p.astype(v.dtype), v)

    out = lax.map(block, (to_query_blocks(q, nb, qb), to_query_blocks(c_q, nb, qb), q_pos.reshape(nb, qb)))
    return from_query_blocks(out)


def attn_mixer(h, w_in, b_f, w_out, past):
    B, T, _ = h.shape
    ws = H_SB * HD_SB
    wf = H_FOX * HD_FOX
    proj = h @ w_in
    q_sb, k_sb, v_sb, q_fx, k_fx, v_fx, f_pre = jnp.split(
        proj, [ws, 2 * ws, 3 * ws, 3 * ws + wf, 3 * ws + 2 * wf, 3 * ws + 3 * wf], axis=-1)
    q_sb = q_sb.reshape(B, T, H_SB, HD_SB)
    k_sb = k_sb.reshape(B, T, H_SB, HD_SB)
    v_sb = v_sb.reshape(B, T, H_SB, HD_SB)
    q_fx = q_fx.reshape(B, T, H_FOX, HD_FOX)
    k_fx = k_fx.reshape(B, T, H_FOX, HD_FOX)
    v_fx = v_fx.reshape(B, T, H_FOX, HD_FOX)
    lf = jax.nn.log_sigmoid((f_pre + b_f).astype(jnp.float32))
    if past is None:
        P = 0
        ksb_all, vsb_all, kfx_all, vfx_all, lf_all = k_sb, v_sb, k_fx, v_fx, lf
    else:
        pk_sb, pv_sb, pk_fx, pv_fx, plf = past
        P = pk_sb.shape[1]
        ksb_all = jnp.concatenate([pk_sb.astype(k_sb.dtype), k_sb], axis=1)
        vsb_all = jnp.concatenate([pv_sb.astype(v_sb.dtype), v_sb], axis=1)
        kfx_all = jnp.concatenate([pk_fx.astype(k_fx.dtype), k_fx], axis=1)
        vfx_all = jnp.concatenate([pv_fx.astype(v_fx.dtype), v_fx], axis=1)
        lf_all = jnp.concatenate([plf.astype(jnp.float32), lf], axis=1)
    q_pos = jnp.arange(T, dtype=jnp.int32) + P
    k_pos = jnp.arange(P + T, dtype=jnp.int32)
    c = jnp.cumsum(lf_all, axis=1)
    o_sb = stick_breaking_attention(q_sb, ksb_all, vsb_all, q_pos, k_pos)
    o_fx = forgetting_attention(q_fx, kfx_all, vfx_all, c[:, P:], c, q_pos, k_pos)
    out = jnp.concatenate([o_sb.reshape(B, T, ws), o_fx.reshape(B, T, wf)], axis=-1) @ w_out
    return out, (k_sb, v_sb, k_fx, v_fx, lf.astype(h.dtype))


def mlstm_chunkwise(q, k, v, i_pre, lf, C0, n0, m0):
    B, T, H, DK = q.shape
    DV = v.shape[-1]
    L = math.gcd(T, MLSTM_CHUNK)
    NC = T // L

    def to_chunks(a):
        return jnp.moveaxis(a.reshape(B, NC, L, H, *a.shape[3:]), (1, 3), (0, 2))

    causal = jnp.tril(jnp.ones((L, L), dtype=bool))

    def step(carry, xs):
        C, n, m = carry
        qc, kc, vc, ic, fc = xs
        b = jnp.cumsum(fc, axis=-1)
        D = jnp.where(causal, b[..., :, None] - b[..., None, :] + ic[..., None, :], -jnp.inf)
        g = b + m[..., None]
        m_t = jnp.maximum(g, jnp.max(D, axis=-1))
        Wt = jnp.exp(D - m_t[..., None]) * jnp.einsum('bhtk,bhsk->bhts', qc, kc)
        decay = jnp.exp(g - m_t)
        num = decay[..., None] * jnp.einsum('bhvk,bhtk->bhtv', C, qc) + jnp.einsum('bhts,bhsv->bhtv', Wt, vc)
        den = decay * jnp.einsum('bhk,bhtk->bht', n, qc) + jnp.sum(Wt, axis=-1)
        h = num / jnp.maximum(jnp.abs(den), jnp.exp(-m_t))[..., None]
        bL = b[..., -1]
        a_s = bL[..., None] - b + ic
        m_new = jnp.maximum(bL + m, jnp.max(a_s, axis=-1))
        w = jnp.exp(a_s - m_new[..., None])
        carry_decay = jnp.exp(bL + m - m_new)
        C_new = carry_decay[..., None, None] * C + jnp.einsum('bhs,bhsv,bhsk->bhvk', w, vc, kc)
        n_new = carry_decay[..., None] * n + jnp.einsum('bhs,bhsk->bhk', w, kc)
        return (C_new, n_new, m_new), h

    (C, n, m), hs = lax.scan(step, (C0, n0, m0),
                             (to_chunks(q), to_chunks(k), to_chunks(v), to_chunks(i_pre), to_chunks(lf)))
    hs = jnp.moveaxis(hs, (0, 2), (1, 3)).reshape(B, T, H, DV)
    return hs, (C, n, m)


def mlstm_mixer(h, w_in, b_i, b_f, norm_g, w_out, C0, n0, m0):
    B, T, _ = h.shape
    a = H_ML * DK_ML
    c = H_ML * DV_ML
    proj = h @ w_in
    q, k, v, o, i_pre, f_pre = jnp.split(proj, [a, 2 * a, 2 * a + c, 2 * a + 2 * c, 2 * a + 2 * c + H_ML], axis=-1)
    q = q.reshape(B, T, H_ML, DK_ML).astype(jnp.float32)
    k = k.reshape(B, T, H_ML, DK_ML).astype(jnp.float32) * (DK_ML ** -0.5)
    v = v.reshape(B, T, H_ML, DV_ML).astype(jnp.float32)
    i_log = (i_pre + b_i).astype(jnp.float32)
    lf = jax.nn.log_sigmoid((f_pre + b_f).astype(jnp.float32))
    hs, (C, n, m) = mlstm_chunkwise(q, k, v, i_log, lf, C0.astype(jnp.float32),
                                    n0.astype(jnp.float32), m0.astype(jnp.float32))
    hn = hs * lax.rsqrt(jnp.mean(hs * hs, axis=-1, keepdims=True) + NORM_EPS) * norm_g.astype(jnp.float32)
    gated = jax.nn.sigmoid(o.reshape(B, T, H_ML, DV_ML).astype(jnp.float32)) * hn
    out = gated.astype(h.dtype).reshape(B, T, H_ML * DV_ML) @ w_out
    return out, (C.astype(C0.dtype), n.astype(n0.dtype), m.astype(m0.dtype))


def trunk(x, att_cache, page_table, ml_state, norm_g, ffn1_w_in, ffn1_w_out, ffn2_w_in, ffn2_w_out,
          att_w_in, att_b_f, att_w_out, ml_w_in, ml_b_i, ml_b_f, ml_norm_g, ml_w_out):
    att_rows = []
    ml_rows = []
    for layer in range(DEPTH):
        idx = layer // 2
        x = x + 0.5 * rms_norm(swiglu(rms_norm(x, norm_g[layer, 0]), ffn1_w_in[layer], ffn1_w_out[layer]),
                               norm_g[layer, 1])
        h = rms_norm(x, norm_g[layer, 2])
        if layer % 2 == 0:
            past = None if att_cache is None else [gather_pages(cp[idx], page_table) for cp in att_cache]
            mix, rows = attn_mixer(h, att_w_in[idx], att_b_f[idx], att_w_out[idx], past)
            att_rows.append(rows)
        else:
            if ml_state is None:
                B = x.shape[0]
                C0 = jnp.zeros((B, H_ML, DV_ML, DK_ML), x.dtype)
                n0 = jnp.zeros((B, H_ML, DK_ML), x.dtype)
                m0 = jnp.zeros((B, H_ML), x.dtype)
            else:
                C0, n0, m0 = ml_state[0][idx], ml_state[1][idx], ml_state[2][idx]
            mix, st = mlstm_mixer(h, ml_w_in[idx], ml_b_i[idx], ml_b_f[idx], ml_norm_g[idx], ml_w_out[idx],
                                  C0, n0, m0)
            ml_rows.append(st)
        x = x + rms_norm(mix, norm_g[layer, 3])
        x = x + 0.5 * rms_norm(swiglu(rms_norm(x, norm_g[layer, 4]), ffn2_w_in[layer], ffn2_w_out[layer]),
                               norm_g[layer, 5])
    att_new = [jnp.stack([r[i] for r in att_rows]) for i in range(5)]
    ml_new = [jnp.stack([r[i] for r in ml_rows]) for i in range(3)]
    return x, att_new, ml_new


def setup_inputs(seed: int = 0) -> dict:
    key = jax.random.key(seed)
    ks = jax.random.split(key, 32)
    n_pages = PAST_LEN // PAGE_SIZE
    n_used = DEC_BATCH * n_pages
    n_phys = n_used + max(1, n_used // 4)

    def nrm(k, shape, scale):
        return jax.random.normal(k, shape, jnp.float32) * scale

    kv_sb = (N_ATT_LAYERS, n_phys, PAGE_SIZE, H_SB, HD_SB)
    kv_fx = (N_ATT_LAYERS, n_phys, PAGE_SIZE, H_FOX, HD_FOX)
    page_table = jax.random.permutation(ks[7], n_phys)[:n_used].reshape(DEC_BATCH, n_pages).astype(jnp.int32)
    return {
        'x_prompt': nrm(ks[0], (BATCH, SEQ, D_MODEL), 1.0),
        'x_sample': nrm(ks[1], (DEC_BATCH, DEC_SEQ, D_MODEL), 1.0),
        'cache_sb_k': nrm(ks[2], kv_sb, 1.0),
        'cache_sb_v': nrm(ks[3], kv_sb, 1.0),
        'cache_fox_k': nrm(ks[4], kv_fx, 1.0),
        'cache_fox_v': nrm(ks[5], kv_fx, 1.0),
        'cache_fox_logf': jax.nn.log_sigmoid(nrm(ks[6], (N_ATT_LAYERS, n_phys, PAGE_SIZE, H_FOX), 1.0)),
        'state_mlstm_C': nrm(ks[8], (N_ML_LAYERS, DEC_BATCH, H_ML, DV_ML, DK_ML), 0.1),
        'state_mlstm_n': nrm(ks[9], (N_ML_LAYERS, DEC_BATCH, H_ML, DK_ML), 0.1),
        'state_mlstm_m': nrm(ks[10], (N_ML_LAYERS, DEC_BATCH, H_ML), 1.0),
        'page_table': page_table,
        'norm_g': 1.0 + nrm(ks[11], (DEPTH, 6, D_MODEL), 0.1),
        'ffn1_w_in': nrm(ks[12], (DEPTH, D_MODEL, 2 * D_FF), D_MODEL ** -0.5),
        'ffn1_w_out': nrm(ks[13], (DEPTH, D_FF, D_MODEL), D_FF ** -0.5),
        'ffn2_w_in': nrm(ks[14], (DEPTH, D_MODEL, 2 * D_FF), D_MODEL ** -0.5),
        'ffn2_w_out': nrm(ks[15], (DEPTH, D_FF, D_MODEL), D_FF ** -0.5),
        'att_w_in': nrm(ks[16], (N_ATT_LAYERS, D_MODEL, ATT_IN), D_MODEL ** -0.5),
        'att_b_f': nrm(ks[17], (N_ATT_LAYERS, H_FOX), 0.1),
        'att_w_out': nrm(ks[18], (N_ATT_LAYERS, H_SB * HD_SB + H_FOX * HD_FOX, D_MODEL),
                         (H_SB * HD_SB + H_FOX * HD_FOX) ** -0.5),
        'ml_w_in': nrm(ks[19], (N_ML_LAYERS, D_MODEL, ML_IN), D_MODEL ** -0.5),
        'ml_b_i': nrm(ks[20], (N_ML_LAYERS, H_ML), 0.1),
        'ml_b_f': ML_FORGET_BIAS + nrm(ks[21], (N_ML_LAYERS, H_ML), 0.1),
        'ml_norm_g': 1.0 + nrm(ks[22], (N_ML_LAYERS, H_ML, DV_ML), 0.1),
        'ml_w_out': nrm(ks[23], (N_ML_LAYERS, H_ML * DV_ML, D_MODEL), (H_ML * DV_ML) ** -0.5),
    }


def reference(x_prompt, x_sample, cache_sb_k, cache_sb_v, cache_fox_k, cache_fox_v, cache_fox_logf,
              state_mlstm_C, state_mlstm_n, state_mlstm_m, page_table, norm_g, ffn1_w_in, ffn1_w_out,
              ffn2_w_in, ffn2_w_out, att_w_in, att_b_f, att_w_out, ml_w_in, ml_b_i, ml_b_f, ml_norm_g,
              ml_w_out):
    y_prompt, att_p, ml_p = trunk(x_prompt, None, None, None, norm_g, ffn1_w_in, ffn1_w_out, ffn2_w_in,
                                  ffn2_w_out, att_w_in, att_b_f, att_w_out, ml_w_in, ml_b_i, ml_b_f,
                                  ml_norm_g, ml_w_out)
    y_sample, att_s, ml_s = trunk(x_sample, (cache_sb_k, cache_sb_v, cache_fox_k, cache_fox_v, cache_fox_logf),
                                  page_table, (state_mlstm_C, state_mlstm_n, state_mlstm_m), norm_g,
                                  ffn1_w_in, ffn1_w_out, ffn2_w_in, ffn2_w_out, att_w_in, att_b_f, att_w_out,
                                  ml_w_in, ml_b_i, ml_b_f, ml_norm_g, ml_w_out)
    return (y_prompt, y_sample,
            att_p[0], att_p[1], att_p[2], att_p[3], att_p[4], ml_p[0], ml_p[1], ml_p[2],
            att_s[0], att_s[1], att_s[2], att_s[3], att_s[4], ml_s[0], ml_s[1], ml_s[2])
```

```python
import functools

import jax
import jax.numpy as jnp
from jax import lax
from jax.experimental import pallas as pl
from jax.experimental.pallas import tpu as pltpu

NORM_EPS = 1e-6
LANES = 128
SUBLANES = 8
VMEM_LIMIT_BYTES = 56 * 1024 * 1024

F32 = jnp.float32
BF16 = jnp.bfloat16
NT_DIMS = (((1,), (1,)), ((), ()))


def _rms(x, g):
    return x * lax.rsqrt(jnp.mean(x * x, axis=-1, keepdims=True) + NORM_EPS) * g


def _log_sigmoid(x):
    return jnp.minimum(x, 0.0) - jnp.log(1.0 + jnp.exp(-jnp.abs(x)))


def _sigmoid(x):
    return 1.0 / (1.0 + jnp.exp(-x))


def _resident(shape):
    return pl.BlockSpec(shape, lambda *_: (0,) * len(shape), pipeline_mode=pl.Buffered(1))


def _params(semantics):
    return pltpu.CompilerParams(dimension_semantics=semantics, vmem_limit_bytes=VMEM_LIMIT_BYTES)


def _ffn_kernel(x_ref, gpre_ref, win_ref, wout_ref, gpost_ref, o_ref, acc_ref, *, d_ff, tf):
    x = x_ref[...]
    h = _rms(x, gpre_ref[...]).astype(BF16)
    for c in range(d_ff // tf):
        gate = jnp.dot(h, win_ref[:, c * tf:(c + 1) * tf], preferred_element_type=F32)
        up = jnp.dot(h, win_ref[:, d_ff + c * tf:d_ff + (c + 1) * tf], preferred_element_type=F32)
        act = (gate * _sigmoid(gate) * up).astype(BF16)
        part = jnp.dot(act, wout_ref[c * tf:(c + 1) * tf, :], preferred_element_type=F32)
        if c == 0:
            acc_ref[...] = part
        else:
            acc_ref[...] += part
    o_ref[...] = x + 0.5 * _rms(acc_ref[...], gpost_ref[...])


def _ffn(x, g_pre, w_in, w_out, g_post):
    m, d = x.shape
    d_ff = w_out.shape[0]
    tm = min(m, 512)
    tf = 256
    assert m % tm == 0 and d_ff % tf == 0
    return pl.pallas_call(
        functools.partial(_ffn_kernel, d_ff=d_ff, tf=tf),
        grid=(m // tm,),
        in_specs=[pl.BlockSpec((tm, d), lambda i: (i, 0)),
                  _resident((1, d)), _resident((d, 2 * d_ff)), _resident((d_ff, d)), _resident((1, d))],
        out_specs=pl.BlockSpec((tm, d), lambda i: (i, 0)),
        out_shape=jax.ShapeDtypeStruct((m, d), F32),
        scratch_shapes=[pltpu.VMEM((tm, d), F32)],
        compiler_params=_params(("parallel",)),
        name="ffn",
    )(x, g_pre, w_in, w_out, g_post)


def _proj_kernel(x_ref, g_ref, w_ref, gbias_ref, glogsig_ref, o_ref, *, n_main, chunk):
    h = _rms(x_ref[...], g_ref[...]).astype(BF16)
    for c0 in range(0, n_main, chunk):
        c1 = min(c0 + chunk, n_main)
        o_ref[:, c0:c1] = jnp.dot(h, w_ref[:, c0:c1], preferred_element_type=F32)
    y = jnp.dot(h, w_ref[:, n_main:n_main + LANES], preferred_element_type=F32) + gbias_ref[...]
    o_ref[:, n_main:n_main + LANES] = jnp.where(glogsig_ref[...] > 0.0, _log_sigmoid(y), y)


def _proj(x, g, w, gate_bias, gate_logsig):
    m, d = x.shape
    n = w.shape[1]
    n_main = n - LANES
    tm = min(m, 512)
    assert m % tm == 0
    return pl.pallas_call(
        functools.partial(_proj_kernel, n_main=n_main, chunk=512),
        grid=(m // tm,),
        in_specs=[pl.BlockSpec((tm, d), lambda i: (i, 0)),
                  _resident((1, d)), _resident((d, n)), _resident((1, LANES)), _resident((1, LANES))],
        out_specs=pl.BlockSpec((tm, n), lambda i: (i, 0)),
        out_shape=jax.ShapeDtypeStruct((m, n), F32),
        compiler_params=_params(("parallel",)),
        name="proj",
    )(x, g, w, gate_bias, gate_logsig)


def _outproj_kernel(*refs, n_in):
    a_refs, w_refs = refs[:n_in], refs[n_in:2 * n_in]
    g_ref, x_ref, o_ref = refs[2 * n_in:]
    mix = None
    for a_ref, w_ref in zip(a_refs, w_refs):
        part = jnp.dot(a_ref[...].astype(BF16), w_ref[...], preferred_element_type=F32)
        mix = part if mix is None else mix + part
    o_ref[...] = x_ref[...] + _rms(mix, g_ref[...])


def _outproj(acts, ws, g, x):
    m, d = x.shape
    tm = min(m, 512)
    assert m % tm == 0
    n_in = len(acts)
    in_specs = [pl.BlockSpec((tm, a.shape[1]), lambda i: (i, 0)) for a in acts]
    in_specs += [_resident(w.shape) for w in ws]
    in_specs += [_resident((1, d)), pl.BlockSpec((tm, d), lambda i: (i, 0))]
    return pl.pallas_call(
        functools.partial(_outproj_kernel, n_in=n_in),
        grid=(m // tm,),
        in_specs=in_specs,
        out_specs=pl.BlockSpec((tm, d), lambda i: (i, 0)),
        out_shape=jax.ShapeDtypeStruct((m, d), F32),
        compiler_params=_params(("parallel",)),
        name="outproj",
    )(*acts, *ws, g, x)


def _head_lane_mask(hd, h):
    lane = lax.broadcasted_iota(jnp.int32, (1, LANES), 1)
    return (lane >= h * hd) & (lane < (h + 1) * hd)


def _sb_prefill_kernel(q_ref, k_ref, v_ref, o_ref, *, blk, hd, scale):
    i = pl.program_id(2)
    heads = LANES // hd
    q = q_ref[...] * scale
    qh = [jnp.where(_head_lane_mask(hd, h), q, 0.0).astype(BF16) for h in range(heads)]
    row = lax.broadcasted_iota(jnp.int32, (blk, blk), 0)
    col = lax.broadcasted_iota(jnp.int32, (blk, blk), 1)
    later_sel = (row > col).astype(BF16)
    causal = col < row

    def block(j, carries, acc, masked):
        start = pl.multiple_of(j * blk, blk)
        kb = k_ref[pl.ds(start, blk), :].astype(BF16)
        vb = v_ref[pl.ds(start, blk), :]
        new_carries = []
        for h in range(heads):
            z = lax.dot_general(qh[h], kb, NT_DIMS, preferred_element_type=F32)
            log_beta = _log_sigmoid(z)
            log_1m = log_beta - z
            if masked:
                log_1m = jnp.where(causal, log_1m, 0.0)
            hi = log_1m.astype(BF16)
            lo = (log_1m - hi.astype(F32)).astype(BF16)
            later = (jnp.dot(hi, later_sel, preferred_element_type=F32)
                     + jnp.dot(lo, later_sel, preferred_element_type=F32))
            w = jnp.exp(log_beta + later + carries[h])
            if masked:
                w = jnp.where(causal, w, 0.0)
            vh = jnp.where(_head_lane_mask(hd, h), vb, 0.0).astype(BF16)
            acc = acc + jnp.dot(w.astype(BF16), vh, preferred_element_type=F32)
            new_carries.append(carries[h] + jnp.sum(log_1m, axis=-1, keepdims=True))
        return tuple(new_carries), acc

    zeros = tuple(jnp.zeros((blk, 1), F32) for _ in range(heads))
    carries, acc = block(i, zeros, jnp.zeros((blk, LANES), F32), True)

    def body(t, state):
        return block(i - 1 - t, state[0], state[1], False)

    carries, acc = lax.fori_loop(0, i, body, (carries, acc))
    o_ref[...] = acc


def _sb_prefill(proj, batch, seq, q_col, k_col, v_col, n_groups, hd):
    blk = min(seq, 256)
    nq = seq // blk
    assert seq % blk == 0
    return pl.pallas_call(
        functools.partial(_sb_prefill_kernel, blk=blk, hd=hd, scale=hd ** -0.5),
        grid=(batch, n_groups, nq),
        in_specs=[pl.BlockSpec((blk, LANES), lambda b, p, i: (b * nq + i, q_col + p)),
                  pl.BlockSpec((seq, LANES), lambda b, p, i: (b, k_col + p)),
                  pl.BlockSpec((seq, LANES), lambda b, p, i: (b, v_col + p))],
        out_specs=pl.BlockSpec((blk, LANES), lambda b, p, i: (b * nq + i, p)),
        out_shape=jax.ShapeDtypeStruct((batch * seq, n_groups * LANES), F32),
        compiler_params=_params(("parallel", "parallel", "arbitrary")),
        name="sb_prefill",
    )(proj, proj, proj)


def _cumsum_kernel(x_ref, o_ref, *, chunk):
    rows, t = x_ref.shape
    j = lax.broadcasted_iota(jnp.int32, (chunk, chunk), 0)
    s = lax.broadcasted_iota(jnp.int32, (chunk, chunk), 1)
    incl = (j <= s).astype(F32)
    carry = jnp.zeros((rows, 1), F32)
    for c in range(t // chunk):
        x = x_ref[:, c * chunk:(c + 1) * chunk]
        o_ref[:, c * chunk:(c + 1) * chunk] = carry + jnp.dot(
            x, incl, preferred_element_type=F32, precision=lax.Precision.HIGHEST)
        carry = carry + jnp.sum(x, axis=-1, keepdims=True)


def _cumsum_lanes(x):
    b, r, t = x.shape
    chunk = min(t, 256)
    assert t % chunk == 0
    return pl.pallas_call(
        functools.partial(_cumsum_kernel, chunk=chunk),
        grid=(b,),
        in_specs=[pl.BlockSpec((None, r, t), lambda i: (i, 0, 0))],
        out_specs=pl.BlockSpec((None, r, t), lambda i: (i, 0, 0)),
        out_shape=jax.ShapeDtypeStruct((b, r, t), F32),
        compiler_params=_params(("parallel",)),
        name="cumsum",
    )(x)


def _fox_prefill_kernel(q_ref, k_ref, v_ref, cq_ref, ck_ref, o_ref, *, blk, hd, scale):
    i = pl.program_id(2)
    heads = LANES // hd
    q = q_ref[...] * scale
    qh = [jnp.where(_head_lane_mask(hd, h), q, 0.0).astype(BF16) for h in range(heads)]
    cq = [cq_ref[:, h:h + 1] for h in range(heads)]
    row = lax.broadcasted_iota(jnp.int32, (blk, blk), 0)
    col = lax.broadcasted_iota(jnp.int32, (blk, blk), 1)
    causal = col <= row

    def scores(j, h, kb):
        start = pl.multiple_of(j * blk, blk)
        z = lax.dot_general(qh[h], kb, NT_DIMS, preferred_element_type=F32)
        return z + cq[h] - ck_ref[h:h + 1, pl.ds(start, blk)]

    def load(j):
        start = pl.multiple_of(j * blk, blk)
        return k_ref[pl.ds(start, blk), :].astype(BF16), v_ref[pl.ds(start, blk), :]

    kb, vb = load(i)
    ms, ls, accs = [], [], []
    for h in range(heads):
        s = jnp.where(causal, scores(i, h, kb), -jnp.inf)
        m = jnp.max(s, axis=-1, keepdims=True)
        p = jnp.exp(s - m)
        vh = jnp.where(_head_lane_mask(hd, h), vb, 0.0).astype(BF16)
        ms.append(m)
        ls.append(jnp.sum(p, axis=-1, keepdims=True))
        accs.append(jnp.dot(p.astype(BF16), vh, preferred_element_type=F32))

    def body(t, state):
        ms, ls, accs = state
        j = i - 1 - t
        kb, vb = load(j)
        new_ms, new_ls, new_accs = [], [], []
        for h in range(heads):
            s = scores(j, h, kb)
            m_new = jnp.maximum(ms[h], jnp.max(s, axis=-1, keepdims=True))
            alpha = jnp.exp(ms[h] - m_new)
            p = jnp.exp(s - m_new)
            vh = jnp.where(_head_lane_mask(hd, h), vb, 0.0).astype(BF16)
            new_ms.append(m_new)
            new_ls.append(alpha * ls[h] + jnp.sum(p, axis=-1, keepdims=True))
            new_accs.append(alpha * accs[h] + jnp.dot(p.astype(BF16), vh, preferred_element_type=F32))
        return tuple(new_ms), tuple(new_ls), tuple(new_accs)

    ms, ls, accs = lax.fori_loop(0, i, body, (tuple(ms), tuple(ls), tuple(accs)))
    out = None
    for h in range(heads):
        part = accs[h] / ls[h]
        out = part if out is None else out + part
    o_ref[...] = out


def _fox_prefill(proj, c, batch, seq, q_col, k_col, v_col, n_groups, hd):
    blk = min(seq, 256)
    nq = seq // blk
    heads = LANES // hd
    ck = c.reshape(batch, n_groups, heads, seq)
    cq = jnp.swapaxes(ck, 2, 3)
    return pl.pallas_call(
        functools.partial(_fox_prefill_kernel, blk=blk, hd=hd, scale=hd ** -0.5),
        grid=(batch, n_groups, nq),
        in_specs=[pl.BlockSpec((blk, LANES), lambda b, p, i: (b * nq + i, q_col + p)),
                  pl.BlockSpec((seq, LANES), lambda b, p, i: (b, k_col + p)),
                  pl.BlockSpec((seq, LANES), lambda b, p, i: (b, v_col + p)),
                  pl.BlockSpec((None, None, blk, heads), lambda b, p, i: (b, p, i, 0)),
                  pl.BlockSpec((None, None, heads, seq), lambda b, p, i: (b, p, 0, 0))],
        out_specs=pl.BlockSpec((blk, LANES), lambda b, p, i: (b * nq + i, p)),
        out_shape=jax.ShapeDtypeStruct((batch * seq, n_groups * LANES), F32),
        compiler_params=_params(("parallel", "parallel", "arbitrary")),
        name="fox_prefill",
    )(proj, proj, proj, cq, ck)


def _mlstm_chunk_kernel(q_ref, k_ref, v_ref, og_ref, icol_ref, fcol_ref, irow_ref, frow_ref, ng_ref,
                        c0_ref, n0_ref, m0_ref, gated_ref, c_ref, n_ref, m_ref, *, chunk, scale):
    step = pl.program_id(2)

    @pl.when(step == 0)
    def _():
        c_ref[...] = c0_ref[...]
        n_ref[...] = n0_ref[...]
        m_ref[...] = m0_ref[...]

    q = q_ref[...]
    k = k_ref[...] * scale
    v = v_ref[...]
    qb, kb = q.astype(BF16), k.astype(BF16)
    t_idx = lax.broadcasted_iota(jnp.int32, (chunk, chunk), 0)
    s_idx = lax.broadcasted_iota(jnp.int32, (chunk, chunk), 1)
    causal = s_idx <= t_idx
    b_col = jnp.sum(jnp.where(causal, frow_ref[...], 0.0), axis=1, keepdims=True)
    b_row = jnp.sum(jnp.where(t_idx <= s_idx, fcol_ref[...], 0.0), axis=0, keepdims=True)
    m_prev = m_ref[...]
    n_prev = n_ref[...]
    c_prev = c_ref[...]

    d = jnp.where(causal, b_col - b_row + irow_ref[...], -jnp.inf)
    g = b_col + m_prev
    m_t = jnp.maximum(g, jnp.max(d, axis=1, keepdims=True))
    wt = jnp.exp(d - m_t) * lax.dot_general(qb, kb, NT_DIMS, preferred_element_type=F32)
    decay = jnp.exp(g - m_t)
    num = (decay * lax.dot_general(qb, c_prev.astype(BF16), NT_DIMS, preferred_element_type=F32)
           + jnp.dot(wt.astype(BF16), v.astype(BF16), preferred_element_type=F32))
    den = decay * jnp.sum(q * n_prev, axis=1, keepdims=True) + jnp.sum(wt, axis=1, keepdims=True)
    h = num / jnp.maximum(jnp.abs(den), jnp.exp(-m_t))

    hn = _rms(h, ng_ref[...])
    gated_ref[...] = (_sigmoid(og_ref[...]) * hn).astype(gated_ref.dtype)

    b_last = b_col[chunk - 1:chunk, :]
    a_col = b_last - b_col + icol_ref[...]
    m_new = jnp.maximum(b_last + m_prev, jnp.max(a_col, axis=0, keepdims=True))
    w_col = jnp.exp(a_col - m_new)
    carry_decay = jnp.exp(b_last + m_prev - m_new)
    wv_t = (w_col * v).T.astype(BF16)
    c_ref[...] = carry_decay * c_prev + jnp.dot(wv_t, kb, preferred_element_type=F32)
    n_ref[...] = carry_decay * n_prev + jnp.sum(w_col * k, axis=0, keepdims=True)
    m_ref[...] = m_new


def _mlstm_prefill(proj, gates_i, gates_f, norm_g, c0, n0, m0, batch, seq, n_heads, dk, dv):
    chunk = min(seq, 256)
    nc = seq // chunk
    assert seq % chunk == 0 and dk == dv
    kq, kk, kv, ko = 0, n_heads, 2 * n_heads, 3 * n_heads
    row4 = lambda a: a.reshape(batch, n_heads, 1, seq)
    col4 = lambda a: a.reshape(batch, n_heads, seq, 1)
    tok = lambda off: pl.BlockSpec((chunk, dk), lambda b, h, c: (b * nc + c, off + h))
    colspec = pl.BlockSpec((None, None, chunk, 1), lambda b, h, c: (b, h, c, 0))
    rowspec = pl.BlockSpec((None, None, 1, chunk), lambda b, h, c: (b, h, 0, c))
    cspec = pl.BlockSpec((None, None, dv, dk), lambda b, h, c: (b, h, 0, 0))
    nspec = pl.BlockSpec((None, None, 1, dk), lambda b, h, c: (b, h, 0, 0))
    mspec = pl.BlockSpec((None, None, 1, 1), lambda b, h, c: (b, h, 0, 0))
    return pl.pallas_call(
        functools.partial(_mlstm_chunk_kernel, chunk=chunk, scale=dk ** -0.5),
        grid=(batch, n_heads, nc),
        in_specs=[tok(kq), tok(kk), tok(kv), tok(ko), colspec, colspec, rowspec, rowspec,
                  pl.BlockSpec((None, 1, dv), lambda b, h, c: (h, 0, 0)), cspec, nspec, mspec],
        out_specs=[pl.BlockSpec((chunk, dv), lambda b, h, c: (b * nc + c, h)), cspec, nspec, mspec],
        out_shape=[jax.ShapeDtypeStruct((batch * seq, n_heads * dv), BF16),
                   jax.ShapeDtypeStruct((batch, n_heads, dv, dk), F32),
                   jax.ShapeDtypeStruct((batch, n_heads, 1, dk), F32),
                   jax.ShapeDtypeStruct((batch, n_heads, 1, 1), F32)],
        compiler_params=_params(("parallel", "parallel", "arbitrary")),
        name="mlstm_chunk",
    )(proj, proj, proj, proj, col4(gates_i), col4(gates_f), row4(gates_i), row4(gates_f),
      norm_g.reshape(n_heads, 1, dv), c0, n0.reshape(batch, n_heads, 1, dk), m0.reshape(batch, n_heads, 1, 1))


def _block_diag_mask(n_heads, hd):
    r = lax.broadcasted_iota(jnp.int32, (n_heads, n_heads * hd), 0)
    lane = lax.broadcasted_iota(jnp.int32, (n_heads, n_heads * hd), 1)
    return (lane >= r * hd) & (lane < (r + 1) * hd)


def _lanes_to_col(row, n):
    r = lax.broadcasted_iota(jnp.int32, (n, row.shape[1]), 0)
    lane = lax.broadcasted_iota(jnp.int32, (n, row.shape[1]), 1)
    return jnp.sum(jnp.where(r == lane, row, 0.0), axis=1, keepdims=True)


def _decode_attn_kernel(pt_ref, row_ref, ksb_ref, vsb_ref, kfx_ref, vfx_ref, lft_ref, o_ref,
                        qsb_s, qfx_s, carry_sb_s, acc_sb_s, carry_fx_s, m_s, l_s, acc_fx_s,
                        *, n_sb, hd_sb, n_fx, hd_fx, cols):
    del pt_ref
    p = pl.program_id(1)
    page = ksb_ref.shape[1]
    w_sb, w_fx = n_sb * hd_sb, n_fx * hd_fx
    q_sb_col, q_fx_col, k_fx_col, v_fx_col, gate_col = cols
    mask_sb = _block_diag_mask(n_sb, hd_sb)
    mask_fx = _block_diag_mask(n_fx, hd_fx)

    @pl.when(p == 0)
    def _():
        row = row_ref[...]
        q_sb = jnp.where(mask_sb, row[:, q_sb_col:q_sb_col + w_sb] * hd_sb ** -0.5, 0.0)
        q_fx = jnp.where(mask_fx, row[:, q_fx_col:q_fx_col + w_fx] * hd_fx ** -0.5, 0.0)
        qsb_s[...] = q_sb.astype(BF16)
        qfx_s[...] = q_fx.astype(BF16)
        carry_sb_s[...] = jnp.zeros_like(carry_sb_s)
        acc_sb_s[...] = jnp.zeros_like(acc_sb_s)
        k_cur = row[:, k_fx_col:k_fx_col + w_fx]
        v_cur = row[:, v_fx_col:v_fx_col + w_fx]
        m_s[...] = jnp.sum(q_fx * k_cur, axis=1, keepdims=True)
        l_s[...] = jnp.ones_like(l_s)
        acc_fx_s[...] = jnp.broadcast_to(v_cur, acc_fx_s.shape)
        carry_fx_s[...] = _lanes_to_col(row[:, gate_col:gate_col + LANES], n_fx)

    j = lax.broadcasted_iota(jnp.int32, (page, page), 0)
    s = lax.broadcasted_iota(jnp.int32, (page, page), 1)
    later_sel = (j > s).astype(F32)

    def suffix(x):
        return jnp.dot(x, later_sel, preferred_element_type=F32, precision=lax.Precision.HIGHEST)

    z = jnp.dot(qsb_s[...], ksb_ref[...].astype(BF16), preferred_element_type=F32)
    log_beta = _log_sigmoid(z)
    log_1m = log_beta - z
    w = jnp.exp(log_beta + suffix(log_1m) + carry_sb_s[...])
    acc_sb_s[...] += lax.dot_general(w.astype(BF16), vsb_ref[...].astype(BF16), NT_DIMS,
                                     preferred_element_type=F32)
    carry_sb_s[...] += jnp.sum(log_1m, axis=1, keepdims=True)

    lf = lft_ref[...]
    sc = (jnp.dot(qfx_s[...], kfx_ref[...].astype(BF16), preferred_element_type=F32)
          + suffix(lf) + carry_fx_s[...])
    m_new = jnp.maximum(m_s[...], jnp.max(sc, axis=1, keepdims=True))
    alpha = jnp.exp(m_s[...] - m_new)
    pr = jnp.exp(sc - m_new)
    l_s[...] = alpha * l_s[...] + jnp.sum(pr, axis=1, keepdims=True)
    acc_fx_s[...] = alpha * acc_fx_s[...] + lax.dot_general(
        pr.astype(BF16), vfx_ref[...].astype(BF16), NT_DIMS, preferred_element_type=F32)
    m_s[...] = m_new
    carry_fx_s[...] += jnp.sum(lf, axis=1, keepdims=True)

    @pl.when(p == pl.num_programs(1) - 1)
    def _():
        o_sb = jnp.sum(jnp.where(mask_sb, acc_sb_s[...], 0.0), axis=0, keepdims=True)
        o_fx = jnp.sum(jnp.where(mask_fx, acc_fx_s[...] / l_s[...], 0.0), axis=0, keepdims=True)
        o_ref[:, 0:w_sb] = o_sb
        o_ref[:, w_sb:w_sb + w_fx] = o_fx


def _decode_attn(proj, page_table, layer, cache_sb_k, cache_sb_v, cache_fox_k, cache_fox_v, cache_fox_logf, cols):
    b, n = proj.shape
    _, n_phys, page, n_sb, hd_sb = cache_sb_k.shape
    n_fx, hd_fx = cache_fox_k.shape[-2:]
    n_pages = page_table.shape[1]
    w_sb, w_fx = n_sb * hd_sb, n_fx * hd_fx
    flat = lambda a: jnp.transpose(a, (0, 1, 3, 4, 2)).reshape(a.shape[0], n_phys, -1, page)
    lft = jnp.swapaxes(cache_fox_logf, 2, 3)

    def paged(width_or_rows, last):
        return pl.BlockSpec((None, None, width_or_rows, last),
                            lambda i, p, pt: (layer, pt[i, n_pages - 1 - p], 0, 0))

    grid_spec = pltpu.PrefetchScalarGridSpec(
        num_scalar_prefetch=1,
        grid=(b, n_pages),
        in_specs=[pl.BlockSpec((None, 1, n), lambda i, p, pt: (i, 0, 0)),
                  paged(w_sb, page), paged(w_sb, page), paged(w_fx, page), paged(w_fx, page),
                  paged(n_fx, page)],
        out_specs=pl.BlockSpec((None, 1, w_sb + w_fx), lambda i, p, pt: (i, 0, 0)),
        scratch_shapes=[pltpu.VMEM((n_sb, w_sb), BF16), pltpu.VMEM((n_fx, w_fx), BF16),
                        pltpu.VMEM((n_sb, 1), F32), pltpu.VMEM((n_sb, w_sb), F32),
                        pltpu.VMEM((n_fx, 1), F32), pltpu.VMEM((n_fx, 1), F32), pltpu.VMEM((n_fx, 1), F32),
                        pltpu.VMEM((n_fx, w_fx), F32)])
    out = pl.pallas_call(
        functools.partial(_decode_attn_kernel, n_sb=n_sb, hd_sb=hd_sb, n_fx=n_fx, hd_fx=hd_fx, cols=cols),
        grid_spec=grid_spec,
        out_shape=jax.ShapeDtypeStruct((b, 1, w_sb + w_fx), F32),
        compiler_params=_params(("parallel", "arbitrary")),
        name="decode_attn",
    )(page_table, proj.reshape(b, 1, n), flat(cache_sb_k), flat(cache_sb_v), flat(cache_fox_k),
      flat(cache_fox_v), lft)
    return out.reshape(b, w_sb + w_fx)


def _mlstm_step_kernel(row_ref, ng_ref, c0_ref, n0_ref, m0_ref, gated_ref, c_ref, n_ref, m_ref,
                       *, n_heads, dk, dv, gate_col):
    row = row_ref[...]
    gates = row[:, gate_col:gate_col + LANES]
    eye = (lax.broadcasted_iota(jnp.int32, (dv, dv), 0) == lax.broadcasted_iota(jnp.int32, (dv, dv), 1))
    for h in range(n_heads):
        q = row[:, h * dk:(h + 1) * dk]
        k = row[:, (n_heads + h) * dk:(n_heads + h + 1) * dk] * dk ** -0.5
        v = row[:, 2 * n_heads * dk + h * dv:2 * n_heads * dk + (h + 1) * dv]
        og = row[:, 2 * n_heads * dk + (n_heads + h) * dv:2 * n_heads * dk + (n_heads + h + 1) * dv]
        i_log = gates[:, h:h + 1]
        f_log = gates[:, n_heads + h:n_heads + h + 1]
        c_prev = c0_ref[h]
        n_prev = n0_ref[h:h + 1, :]
        m_prev = m0_ref[h:h + 1, :]
        g = f_log + m_prev
        m_t = jnp.maximum(g, i_log)
        w_in = jnp.exp(i_log - m_t)
        decay = jnp.exp(g - m_t)
        qk = jnp.sum(q * k, axis=1, keepdims=True)
        q8 = jnp.broadcast_to(q, (SUBLANES, dk)).astype(BF16)
        cq = lax.dot_general(q8, c_prev.astype(BF16), NT_DIMS, preferred_element_type=F32)[0:1, :]
        num = decay * cq + (w_in * qk) * v
        den = decay * jnp.sum(q * n_prev, axis=1, keepdims=True) + w_in * qk
        hid = num / jnp.maximum(jnp.abs(den), jnp.exp(-m_t))
        hn = _rms(hid, ng_ref[h:h + 1, :])
        gated_ref[:, h * dv:(h + 1) * dv] = _sigmoid(og) * hn
        v_col = jnp.sum(jnp.where(eye, v, 0.0), axis=1, keepdims=True)
        c_ref[h] = decay * c_prev + v_col * (w_in * k)
        n_ref[h:h + 1, :] = decay * n_prev + w_in * k
        m_ref[h:h + 1, :] = m_t


def _mlstm_decode(proj, norm_g, c0, n0, m0, gate_col):
    b, n = proj.shape
    _, n_heads, dv, dk = c0.shape
    seq_spec = lambda *tail: pl.BlockSpec((None,) + tail, lambda i: (i,) + (0,) * len(tail))
    gated, c, nn, m = pl.pallas_call(
        functools.partial(_mlstm_step_kernel, n_heads=n_heads, dk=dk, dv=dv, gate_col=gate_col),
        grid=(b,),
        in_specs=[seq_spec(1, n), _resident((n_heads, dv)), seq_spec(n_heads, dv, dk),
                  seq_spec(n_heads, dk), seq_spec(n_heads, 1)],
        out_specs=[seq_spec(1, n_heads * dv), seq_spec(n_heads, dv, dk), seq_spec(n_heads, dk),
                   seq_spec(n_heads, 1)],
        out_shape=[jax.ShapeDtypeStruct((b, 1, n_heads * dv), F32),
                   jax.ShapeDtypeStruct((b, n_heads, dv, dk), F32),
                   jax.ShapeDtypeStruct((b, n_heads, dk), F32),
                   jax.ShapeDtypeStruct((b, n_heads, 1), F32)],
        compiler_params=_params(("parallel",)),
        name="mlstm_step",
    )(proj.reshape(b, 1, n), norm_g, c0, n0, m0.reshape(b, n_heads, 1))
    return gated.reshape(b, n_heads * dv), c, nn, m.reshape(b, n_heads)


def _pad_gate_cols(w, n_main):
    n_gates = w.shape[1] - n_main
    return jnp.pad(w, ((0, 0), (0, LANES - n_gates))).astype(BF16)


def _gate_rows(bias, logsig_from):
    n = bias.shape[0]
    gb = jnp.pad(bias.astype(F32), (0, LANES - n)).reshape(1, LANES)
    gm = (jnp.arange(LANES) >= logsig_from) & (jnp.arange(LANES) < n)
    return gb, gm.astype(F32).reshape(1, LANES)


def _trunk(x, caches, page_table, ml_state, p):
    bsz, seq, d = x.shape
    m = bsz * seq
    x = x.reshape(m, d)
    depth = p["norm_g"].shape[0]
    n_sb, hd_sb, n_fx, hd_fx = p["n_sb"], p["hd_sb"], p["n_fx"], p["hd_fx"]
    w_sb, w_fx = n_sb * hd_sb, n_fx * hd_fx
    n_ml, dk, dv = p["n_ml"], p["dk"], p["dv"]
    att_rows, ml_rows = [], []
    for layer in range(depth):
        idx = layer // 2
        g = lambda j: p["norm_g"][layer, j].reshape(1, d)
        x = _ffn(x, g(0), p["ffn1_w_in"][layer], p["ffn1_w_out"][layer], g(1))
        if layer % 2 == 0:
            gb, gm = _gate_rows(p["att_b_f"][idx], 0)
            proj = _proj(x, g(2), p["att_w_in"][idx], gb, gm)
            main = 3 * w_sb + 3 * w_fx
            k_sb = proj[:, w_sb:2 * w_sb].reshape(bsz, seq, n_sb, hd_sb)
            v_sb = proj[:, 2 * w_sb:3 * w_sb].reshape(bsz, seq, n_sb, hd_sb)
            k_fx = proj[:, 3 * w_sb + w_fx:3 * w_sb + 2 * w_fx].reshape(bsz, seq, n_fx, hd_fx)
            v_fx = proj[:, 3 * w_sb + 2 * w_fx:main].reshape(bsz, seq, n_fx, hd_fx)
            lf = proj[:, main:main + n_fx].reshape(bsz, seq, n_fx)
            att_rows.append((k_sb, v_sb, k_fx, v_fx, lf))
            w_out = p["att_w_out"][idx]
            if caches is None:
                cb = lambda off: off // LANES
                o_sb = _sb_prefill(proj, bsz, seq, cb(0), cb(w_sb), cb(2 * w_sb), w_sb // LANES, hd_sb)
                c = _cumsum_lanes(jnp.swapaxes(lf, 1, 2))
                o_fx = _fox_prefill(proj, c, bsz, seq, cb(3 * w_sb), cb(3 * w_sb + w_fx),
                                    cb(3 * w_sb + 2 * w_fx), w_fx // LANES, hd_fx)
                x = _outproj([o_sb, o_fx], [w_out[:w_sb], w_out[w_sb:]], g(3), x)
            else:
                assert seq == 1
                cols = (0, 3 * w_sb, 3 * w_sb + w_fx, 3 * w_sb + 2 * w_fx, main)
                o = _decode_attn(proj, page_table, idx, *caches, cols)
                x = _outproj([o], [w_out], g(3), x)
        else:
            gb, gm = _gate_rows(jnp.concatenate([p["ml_b_i"][idx], p["ml_b_f"][idx]]), n_ml)
            proj = _proj(x, g(2), p["ml_w_in"][idx], gb, gm)
            main = 2 * n_ml * dk + 2 * n_ml * dv
            if ml_state is None:
                gates = proj[:, main:main + 2 * n_ml].reshape(bsz, seq, 2 * n_ml)
                gates = jnp.transpose(gates, (0, 2, 1))
                zeros = lambda *s: jnp.zeros(s, F32)
                gated, c_new, n_new, m_new = _mlstm_prefill(
                    proj, gates[:, :n_ml], gates[:, n_ml:], p["ml_norm_g"][idx],
                    zeros(bsz, n_ml, dv, dk), zeros(bsz, n_ml, dk), zeros(bsz, n_ml), bsz, seq, n_ml, dk, dv)
                n_new = n_new.reshape(bsz, n_ml, dk)
                m_new = m_new.reshape(bsz, n_ml)
            else:
                assert seq == 1
                gated, c_new, n_new, m_new = _mlstm_decode(
                    proj, p["ml_norm_g"][idx], ml_state[0][idx], ml_state[1][idx], ml_state[2][idx], main)
            ml_rows.append((c_new, n_new, m_new))
            x = _outproj([gated], [p["ml_w_out"][idx]], g(3), x)
        x = _ffn(x, g(4), p["ffn2_w_in"][layer], p["ffn2_w_out"][layer], g(5))
    att_new = [jnp.stack([r[i] for r in att_rows]) for i in range(5)]
    ml_new = [jnp.stack([r[i] for r in ml_rows]) for i in range(3)]
    return x.reshape(bsz, seq, d), att_new, ml_new


def kernel(x_prompt, x_sample, cache_sb_k, cache_sb_v, cache_fox_k, cache_fox_v, cache_fox_logf,
           state_mlstm_C, state_mlstm_n, state_mlstm_m, page_table, norm_g, ffn1_w_in, ffn1_w_out,
           ffn2_w_in, ffn2_w_out, att_w_in, att_b_f, att_w_out, ml_w_in, ml_b_i, ml_b_f, ml_norm_g,
           ml_w_out):
    n_sb, hd_sb = cache_sb_k.shape[-2:]
    n_fx, hd_fx = cache_fox_k.shape[-2:]
    n_ml, dv, dk = state_mlstm_C.shape[-3:]
    att_main = 3 * n_sb * hd_sb + 3 * n_fx * hd_fx
    ml_main = 2 * n_ml * dk + 2 * n_ml * dv
    p = dict(
        norm_g=norm_g, n_sb=n_sb, hd_sb=hd_sb, n_fx=n_fx, hd_fx=hd_fx, n_ml=n_ml, dk=dk, dv=dv,
        ffn1_w_in=ffn1_w_in.astype(BF16), ffn1_w_out=ffn1_w_out.astype(BF16),
        ffn2_w_in=ffn2_w_in.astype(BF16), ffn2_w_out=ffn2_w_out.astype(BF16),
        att_w_in=jnp.stack([_pad_gate_cols(w, att_main) for w in att_w_in]), att_b_f=att_b_f,
        att_w_out=att_w_out.astype(BF16),
        ml_w_in=jnp.stack([_pad_gate_cols(w, ml_main) for w in ml_w_in]), ml_b_i=ml_b_i, ml_b_f=ml_b_f,
        ml_norm_g=ml_norm_g, ml_w_out=ml_w_out.astype(BF16))
    y_p, att_p, ml_p = _trunk(x_prompt, None, None, None, p)
    caches = (cache_sb_k, cache_sb_v, cache_fox_k, cache_fox_v, cache_fox_logf)
    y_s, att_s, ml_s = _trunk(x_sample, caches, page_table,
                              (state_mlstm_C, state_mlstm_n, state_mlstm_m), p)
    return (y_p, y_s, *att_p, *ml_p, *att_s, *ml_s)
```

```python
import functools

import jax
import jax.numpy as jnp
from jax import lax
from jax.experimental import pallas as pl
from jax.experimental.pallas import tpu as pltpu

NORM_EPS = 1e-6
LANES = 128
SUBLANES = 8
VMEM_LIMIT_BYTES = 56 * 1024 * 1024
DECODE_PAGES_PER_STEP = 8

F32 = jnp.float32
BF16 = jnp.bfloat16
NT_DIMS = (((1,), (1,)), ((), ()))


def _rms(x, g):
    return x * lax.rsqrt(jnp.mean(x * x, axis=-1, keepdims=True) + NORM_EPS) * g


def _log_sigmoid(x):
    return jnp.minimum(x, 0.0) - jnp.log(1.0 + jnp.exp(-jnp.abs(x)))


def _sigmoid(x):
    return 1.0 / (1.0 + jnp.exp(-x))


def _resident(shape):
    return pl.BlockSpec(shape, lambda *_: (0,) * len(shape), pipeline_mode=pl.Buffered(1))


def _params(semantics):
    return pltpu.CompilerParams(dimension_semantics=semantics, vmem_limit_bytes=VMEM_LIMIT_BYTES)


def _ffn_kernel(x_ref, gpre_ref, win_ref, wout_ref, gpost_ref, o_ref, acc_ref, *, d_ff, tf):
    x = x_ref[...]
    h = _rms(x, gpre_ref[...]).astype(BF16)
    for c in range(d_ff // tf):
        gate = jnp.dot(h, win_ref[:, c * tf:(c + 1) * tf], preferred_element_type=F32)
        up = jnp.dot(h, win_ref[:, d_ff + c * tf:d_ff + (c + 1) * tf], preferred_element_type=F32)
        act = (gate * _sigmoid(gate) * up).astype(BF16)
        part = jnp.dot(act, wout_ref[c * tf:(c + 1) * tf, :], preferred_element_type=F32)
        if c == 0:
            acc_ref[...] = part
        else:
            acc_ref[...] += part
    o_ref[...] = x + 0.5 * _rms(acc_ref[...], gpost_ref[...])


def _ffn(x, g_pre, w_in, w_out, g_post):
    m, d = x.shape
    d_ff = w_out.shape[0]
    tm = min(m, 512)
    tf = 256
    assert m % tm == 0 and d_ff % tf == 0
    return pl.pallas_call(
        functools.partial(_ffn_kernel, d_ff=d_ff, tf=tf),
        grid=(m // tm,),
        in_specs=[pl.BlockSpec((tm, d), lambda i: (i, 0)),
                  _resident((1, d)), _resident((d, 2 * d_ff)), _resident((d_ff, d)), _resident((1, d))],
        out_specs=pl.BlockSpec((tm, d), lambda i: (i, 0)),
        out_shape=jax.ShapeDtypeStruct((m, d), F32),
        scratch_shapes=[pltpu.VMEM((tm, d), F32)],
        compiler_params=_params(("parallel",)),
        name="ffn",
    )(x, g_pre, w_in, w_out, g_post)


def _proj_kernel(x_ref, g_ref, w_ref, gbias_ref, glogsig_ref, o_ref, *t_refs, n_main, chunk, t_cols):
    h = _rms(x_ref[...], g_ref[...]).astype(BF16)
    for c0 in range(0, n_main, chunk):
        c1 = min(c0 + chunk, n_main)
        y = jnp.dot(h, w_ref[:, c0:c1], preferred_element_type=F32)
        o_ref[:, c0:c1] = y
        if c0 in t_cols:
            t_refs[t_cols.index(c0)][...] = y.T
    y = jnp.dot(h, w_ref[:, n_main:n_main + LANES], preferred_element_type=F32) + gbias_ref[...]
    o_ref[:, n_main:n_main + LANES] = jnp.where(glogsig_ref[...] > 0.0, _log_sigmoid(y), y)


def _proj(x, g, w, gate_bias, gate_logsig, seq=None, t_cols=()):
    m, d = x.shape
    n = w.shape[1]
    n_main = n - LANES
    chunk = 512
    tm = min(m, 512)
    assert m % tm == 0 and all(c % chunk == 0 for c in t_cols)
    out_specs = [pl.BlockSpec((tm, n), lambda i: (i, 0))]
    out_shape = [jax.ShapeDtypeStruct((m, n), F32)]
    if t_cols:
        per_seq = seq // tm
        assert seq % tm == 0
        out_specs += [pl.BlockSpec((None, chunk, tm), lambda i: (i // per_seq, 0, i % per_seq))] * len(t_cols)
        out_shape += [jax.ShapeDtypeStruct((m // seq, chunk, seq), F32)] * len(t_cols)
    return pl.pallas_call(
        functools.partial(_proj_kernel, n_main=n_main, chunk=chunk, t_cols=tuple(t_cols)),
        grid=(m // tm,),
        in_specs=[pl.BlockSpec((tm, d), lambda i: (i, 0)),
                  _resident((1, d)), _resident((d, n)), _resident((1, LANES)), _resident((1, LANES))],
        out_specs=out_specs,
        out_shape=out_shape,
        compiler_params=_params(("parallel",)),
        name="proj",
    )(x, g, w, gate_bias, gate_logsig)


def _outproj_kernel(*refs, n_in):
    a_refs, w_refs = refs[:n_in], refs[n_in:2 * n_in]
    g_ref, x_ref, o_ref = refs[2 * n_in:]
    mix = None
    for a_ref, w_ref in zip(a_refs, w_refs):
        part = jnp.dot(a_ref[...].astype(BF16), w_ref[...], preferred_element_type=F32)
        mix = part if mix is None else mix + part
    o_ref[...] = x_ref[...] + _rms(mix, g_ref[...])


def _outproj(acts, ws, g, x):
    m, d = x.shape
    tm = min(m, 512)
    assert m % tm == 0
    n_in = len(acts)
    in_specs = [pl.BlockSpec((tm, a.shape[1]), lambda i: (i, 0)) for a in acts]
    in_specs += [_resident(w.shape) for w in ws]
    in_specs += [_resident((1, d)), pl.BlockSpec((tm, d), lambda i: (i, 0))]
    return pl.pallas_call(
        functools.partial(_outproj_kernel, n_in=n_in),
        grid=(m // tm,),
        in_specs=in_specs,
        out_specs=pl.BlockSpec((tm, d), lambda i: (i, 0)),
        out_shape=jax.ShapeDtypeStruct((m, d), F32),
        compiler_params=_params(("parallel",)),
        name="outproj",
    )(*acts, *ws, g, x)


def _head_lane_mask(hd, h):
    lane = lax.broadcasted_iota(jnp.int32, (1, LANES), 1)
    return (lane >= h * hd) & (lane < (h + 1) * hd)


def _sb_prefill_kernel(q_ref, k_ref, v_ref, sel_ref, o_ref, *, tq, tk, hd, scale):
    i = pl.program_id(2)
    heads = LANES // hd
    per_tile = tq // tk
    q = q_ref[...] * scale
    qh = [jnp.where(_head_lane_mask(hd, h), q, 0.0).astype(BF16) for h in range(heads)]
    row = lax.broadcasted_iota(jnp.int32, (tq, tk), 0)
    col = lax.broadcasted_iota(jnp.int32, (tq, tk), 1)

    def block(j, carries, acc, diag_offset):
        start = pl.multiple_of(j * tk, tk)
        kb = k_ref[pl.ds(start, tk), :].astype(BF16)
        vb = v_ref[pl.ds(start, tk), :]
        causal = None if diag_offset is None else (col + diag_offset < row)
        new_carries = []
        for h in range(heads):
            z = lax.dot_general(qh[h], kb, NT_DIMS, preferred_element_type=F32)
            log_beta = _log_sigmoid(z)
            log_1m = log_beta - z
            if causal is not None:
                log_1m = jnp.where(causal, log_1m, 0.0)
            hi = log_1m.astype(BF16)
            lo = (log_1m - hi.astype(F32)).astype(BF16)
            both = jnp.dot(jnp.concatenate([hi, lo], axis=0), sel_ref[...], preferred_element_type=F32)
            later = both[:tq] + both[tq:]
            w = jnp.exp(log_beta + later + carries[h])
            if causal is not None:
                w = jnp.where(causal, w, 0.0)
            vh = jnp.where(_head_lane_mask(hd, h), vb, 0.0).astype(BF16)
            acc = acc + jnp.dot(w.astype(BF16), vh, preferred_element_type=F32)
            new_carries.append(carries[h] + jnp.sum(log_1m, axis=-1, keepdims=True))
        return tuple(new_carries), acc

    carries = tuple(jnp.zeros((tq, 1), F32) for _ in range(heads))
    acc = jnp.zeros((tq, LANES), F32)
    for d in reversed(range(per_tile)):
        carries, acc = block(i * per_tile + d, carries, acc, d * tk)

    def body(t, state):
        return block(i * per_tile - 1 - t, state[0], state[1], None)

    carries, acc = lax.fori_loop(0, i * per_tile, body, (carries, acc))
    o_ref[...] = acc


def _sb_prefill(proj, batch, seq, q_col, k_col, v_col, n_groups, hd):
    tq = min(seq, 1024)
    tk = min(seq, 256)
    nq = seq // tq
    assert seq % tq == 0 and tq % tk == 0
    later_sel = jnp.tril(jnp.ones((tk, tk), BF16), -1)
    return pl.pallas_call(
        functools.partial(_sb_prefill_kernel, tq=tq, tk=tk, hd=hd, scale=hd ** -0.5),
        grid=(batch, n_groups, nq),
        in_specs=[pl.BlockSpec((tq, LANES), lambda b, p, i: (b * nq + i, q_col + p)),
                  pl.BlockSpec((seq, LANES), lambda b, p, i: (b, k_col + p)),
                  pl.BlockSpec((seq, LANES), lambda b, p, i: (b, v_col + p)),
                  _resident((tk, tk))],
        out_specs=pl.BlockSpec((tq, LANES), lambda b, p, i: (b * nq + i, p)),
        out_shape=jax.ShapeDtypeStruct((batch * seq, n_groups * LANES), F32),
        compiler_params=_params(("parallel", "parallel", "arbitrary")),
        name="sb_prefill",
    )(proj, proj, proj, later_sel)


def _cumsum_kernel(x_ref, o_ref, *, chunk):
    rows, t = x_ref.shape
    j = lax.broadcasted_iota(jnp.int32, (chunk, chunk), 0)
    s = lax.broadcasted_iota(jnp.int32, (chunk, chunk), 1)
    incl = (j <= s).astype(F32)
    carry = jnp.zeros((rows, 1), F32)
    for c in range(t // chunk):
        x = x_ref[:, c * chunk:(c + 1) * chunk]
        o_ref[:, c * chunk:(c + 1) * chunk] = carry + jnp.dot(
            x, incl, preferred_element_type=F32, precision=lax.Precision.HIGHEST)
        carry = carry + jnp.sum(x, axis=-1, keepdims=True)


def _cumsum_lanes(x):
    b, r, t = x.shape
    chunk = min(t, 256)
    assert t % chunk == 0
    return pl.pallas_call(
        functools.partial(_cumsum_kernel, chunk=chunk),
        grid=(b,),
        in_specs=[pl.BlockSpec((None, r, t), lambda i: (i, 0, 0))],
        out_specs=pl.BlockSpec((None, r, t), lambda i: (i, 0, 0)),
        out_shape=jax.ShapeDtypeStruct((b, r, t), F32),
        compiler_params=_params(("parallel",)),
        name="cumsum",
    )(x)


def _fox_prefill_kernel(q_ref, k_ref, v_ref, cq_ref, ck_ref, o_ref, *, blk, hd, scale):
    i = pl.program_id(2)
    heads = LANES // hd
    q = q_ref[...] * scale
    qh = [jnp.where(_head_lane_mask(hd, h), q, 0.0).astype(BF16) for h in range(heads)]
    cq = [cq_ref[:, h:h + 1] for h in range(heads)]
    row = lax.broadcasted_iota(jnp.int32, (blk, blk), 0)
    col = lax.broadcasted_iota(jnp.int32, (blk, blk), 1)
    causal = col <= row

    def scores(j, h, kb):
        start = pl.multiple_of(j * blk, blk)
        z = lax.dot_general(qh[h], kb, NT_DIMS, preferred_element_type=F32)
        return z + cq[h] - ck_ref[h:h + 1, pl.ds(start, blk)]

    def load(j):
        start = pl.multiple_of(j * blk, blk)
        return k_ref[pl.ds(start, blk), :].astype(BF16), v_ref[pl.ds(start, blk), :]

    kb, vb = load(i)
    ms, ls, accs = [], [], []
    for h in range(heads):
        s = jnp.where(causal, scores(i, h, kb), -jnp.inf)
        m = jnp.max(s, axis=-1, keepdims=True)
        p = jnp.exp(s - m)
        vh = jnp.where(_head_lane_mask(hd, h), vb, 0.0).astype(BF16)
        ms.append(m)
        ls.append(jnp.sum(p, axis=-1, keepdims=True))
        accs.append(jnp.dot(p.astype(BF16), vh, preferred_element_type=F32))

    def body(t, state):
        ms, ls, accs = state
        j = i - 1 - t
        kb, vb = load(j)
        new_ms, new_ls, new_accs = [], [], []
        for h in range(heads):
            s = scores(j, h, kb)
            m_new = jnp.maximum(ms[h], jnp.max(s, axis=-1, keepdims=True))
            alpha = jnp.exp(ms[h] - m_new)
            p = jnp.exp(s - m_new)
            vh = jnp.where(_head_lane_mask(hd, h), vb, 0.0).astype(BF16)
            new_ms.append(m_new)
            new_ls.append(alpha * ls[h] + jnp.sum(p, axis=-1, keepdims=True))
            new_accs.append(alpha * accs[h] + jnp.dot(p.astype(BF16), vh, preferred_element_type=F32))
        return tuple(new_ms), tuple(new_ls), tuple(new_accs)

    ms, ls, accs = lax.fori_loop(0, i, body, (tuple(ms), tuple(ls), tuple(accs)))
    out = None
    for h in range(heads):
        part = accs[h] / ls[h]
        out = part if out is None else out + part
    o_ref[...] = out


def _fox_prefill(proj, c, batch, seq, q_col, k_col, v_col, n_groups, hd):
    blk = min(seq, 512)
    nq = seq // blk
    heads = LANES // hd
    ck = c.reshape(batch, n_groups, heads, seq)
    cq = jnp.swapaxes(ck, 2, 3)
    return pl.pallas_call(
        functools.partial(_fox_prefill_kernel, blk=blk, hd=hd, scale=hd ** -0.5),
        grid=(batch, n_groups, nq),
        in_specs=[pl.BlockSpec((blk, LANES), lambda b, p, i: (b * nq + i, q_col + p)),
                  pl.BlockSpec((seq, LANES), lambda b, p, i: (b, k_col + p)),
                  pl.BlockSpec((seq, LANES), lambda b, p, i: (b, v_col + p)),
                  pl.BlockSpec((None, None, blk, heads), lambda b, p, i: (b, p, i, 0)),
                  pl.BlockSpec((None, None, heads, seq), lambda b, p, i: (b, p, 0, 0))],
        out_specs=pl.BlockSpec((blk, LANES), lambda b, p, i: (b * nq + i, p)),
        out_shape=jax.ShapeDtypeStruct((batch * seq, n_groups * LANES), F32),
        compiler_params=_params(("parallel", "parallel", "arbitrary")),
        name="fox_prefill",
    )(proj, proj, proj, cq, ck)


def _mlstm_chunk_kernel(q_ref, k_ref, v_ref, og_ref, icol_ref, fcol_ref, irow_ref, frow_ref, ng_ref,
                        c0_ref, n0_ref, m0_ref, gated_ref, c_ref, n_ref, m_ref, *, chunk, scale):
    step = pl.program_id(2)

    @pl.when(step == 0)
    def _():
        c_ref[...] = c0_ref[...]
        n_ref[...] = n0_ref[...]
        m_ref[...] = m0_ref[...]

    q = q_ref[...]
    k = k_ref[...] * scale
    v = v_ref[...]
    qb, kb = q.astype(BF16), k.astype(BF16)
    t_idx = lax.broadcasted_iota(jnp.int32, (chunk, chunk), 0)
    s_idx = lax.broadcasted_iota(jnp.int32, (chunk, chunk), 1)
    causal = s_idx <= t_idx
    b_col = jnp.sum(jnp.where(causal, frow_ref[...], 0.0), axis=1, keepdims=True)
    b_row = jnp.sum(jnp.where(t_idx <= s_idx, fcol_ref[...], 0.0), axis=0, keepdims=True)
    m_prev = m_ref[...]
    n_prev = n_ref[...]
    c_prev = c_ref[...]

    d = jnp.where(causal, b_col - b_row + irow_ref[...], -jnp.inf)
    g = b_col + m_prev
    m_t = jnp.maximum(g, jnp.max(d, axis=1, keepdims=True))
    wt = jnp.exp(d - m_t) * lax.dot_general(qb, kb, NT_DIMS, preferred_element_type=F32)
    decay = jnp.exp(g - m_t)
    num = (decay * lax.dot_general(qb, c_prev.astype(BF16), NT_DIMS, preferred_element_type=F32)
           + jnp.dot(wt.astype(BF16), v.astype(BF16), preferred_element_type=F32))
    den = decay * jnp.sum(q * n_prev, axis=1, keepdims=True) + jnp.sum(wt, axis=1, keepdims=True)
    h = num / jnp.maximum(jnp.abs(den), jnp.exp(-m_t))

    hn = _rms(h, ng_ref[...])
    gated_ref[...] = (_sigmoid(og_ref[...]) * hn).astype(gated_ref.dtype)

    b_last = b_col[chunk - 1:chunk, :]
    a_col = b_last - b_col + icol_ref[...]
    m_new = jnp.maximum(b_last + m_prev, jnp.max(a_col, axis=0, keepdims=True))
    w_col = jnp.exp(a_col - m_new)
    carry_decay = jnp.exp(b_last + m_prev - m_new)
    wv_t = (w_col * v).T.astype(BF16)
    c_ref[...] = carry_decay * c_prev + jnp.dot(wv_t, kb, preferred_element_type=F32)
    n_ref[...] = carry_decay * n_prev + jnp.sum(w_col * k, axis=0, keepdims=True)
    m_ref[...] = m_new


def _mlstm_prefill(proj, gates_i, gates_f, norm_g, c0, n0, m0, batch, seq, n_heads, dk, dv):
    chunk = min(seq, 256)
    nc = seq // chunk
    assert seq % chunk == 0 and dk == dv
    kq, kk, kv, ko = 0, n_heads, 2 * n_heads, 3 * n_heads
    row4 = lambda a: a.reshape(batch, n_heads, 1, seq)
    col4 = lambda a: a.reshape(batch, n_heads, seq, 1)
    tok = lambda off: pl.BlockSpec((chunk, dk), lambda b, h, c: (b * nc + c, off + h))
    colspec = pl.BlockSpec((None, None, chunk, 1), lambda b, h, c: (b, h, c, 0))
    rowspec = pl.BlockSpec((None, None, 1, chunk), lambda b, h, c: (b, h, 0, c))
    cspec = pl.BlockSpec((None, None, dv, dk), lambda b, h, c: (b, h, 0, 0))
    nspec = pl.BlockSpec((None, None, 1, dk), lambda b, h, c: (b, h, 0, 0))
    mspec = pl.BlockSpec((None, None, 1, 1), lambda b, h, c: (b, h, 0, 0))
    return pl.pallas_call(
        functools.partial(_mlstm_chunk_kernel, chunk=chunk, scale=dk ** -0.5),
        grid=(batch, n_heads, nc),
        in_specs=[tok(kq), tok(kk), tok(kv), tok(ko), colspec, colspec, rowspec, rowspec,
                  pl.BlockSpec((None, 1, dv), lambda b, h, c: (h, 0, 0)), cspec, nspec, mspec],
        out_specs=[pl.BlockSpec((chunk, dv), lambda b, h, c: (b * nc + c, h)), cspec, nspec, mspec],
        out_shape=[jax.ShapeDtypeStruct((batch * seq, n_heads * dv), BF16),
                   jax.ShapeDtypeStruct((batch, n_heads, dv, dk), F32),
                   jax.ShapeDtypeStruct((batch, n_heads, 1, dk), F32),
                   jax.ShapeDtypeStruct((batch, n_heads, 1, 1), F32)],
        compiler_params=_params(("parallel", "parallel", "arbitrary")),
        name="mlstm_chunk",
    )(proj, proj, proj, proj, col4(gates_i), col4(gates_f), row4(gates_i), row4(gates_f),
      norm_g.reshape(n_heads, 1, dv), c0, n0.reshape(batch, n_heads, 1, dk), m0.reshape(batch, n_heads, 1, 1))


def _block_diag_mask(n_heads, hd):
    r = lax.broadcasted_iota(jnp.int32, (n_heads, n_heads * hd), 0)
    lane = lax.broadcasted_iota(jnp.int32, (n_heads, n_heads * hd), 1)
    return (lane >= r * hd) & (lane < (r + 1) * hd)


def _lanes_to_col(row, n):
    r = lax.broadcasted_iota(jnp.int32, (n, row.shape[1]), 0)
    lane = lax.broadcasted_iota(jnp.int32, (n, row.shape[1]), 1)
    return jnp.sum(jnp.where(r == lane, row, 0.0), axis=1, keepdims=True)


def _decode_attn_kernel(pt_ref, row_ref, *refs, group, n_sb, hd_sb, n_fx, hd_fx, cols):
    del pt_ref
    ksb_refs, vsb_refs, kfx_refs, vfx_refs, lft_refs = (refs[g * group:(g + 1) * group] for g in range(5))
    o_ref, qsb_s, qfx_s, carry_sb_s, acc_sb_s, carry_fx_s, m_s, l_s, acc_fx_s = refs[5 * group:]
    p = pl.program_id(1)
    page = ksb_refs[0].shape[1]
    w_sb, w_fx = n_sb * hd_sb, n_fx * hd_fx
    q_sb_col, q_fx_col, k_fx_col, v_fx_col, gate_col = cols
    mask_sb = _block_diag_mask(n_sb, hd_sb)
    mask_fx = _block_diag_mask(n_fx, hd_fx)

    @pl.when(p == 0)
    def _():
        row = row_ref[...]
        q_sb = jnp.where(mask_sb, row[:, q_sb_col:q_sb_col + w_sb] * hd_sb ** -0.5, 0.0)
        q_fx = jnp.where(mask_fx, row[:, q_fx_col:q_fx_col + w_fx] * hd_fx ** -0.5, 0.0)
        qsb_s[...] = q_sb.astype(BF16)
        qfx_s[...] = q_fx.astype(BF16)
        carry_sb_s[...] = jnp.zeros_like(carry_sb_s)
        acc_sb_s[...] = jnp.zeros_like(acc_sb_s)
        k_cur = row[:, k_fx_col:k_fx_col + w_fx]
        v_cur = row[:, v_fx_col:v_fx_col + w_fx]
        m_s[...] = jnp.sum(q_fx * k_cur, axis=1, keepdims=True)
        l_s[...] = jnp.ones_like(l_s)
        acc_fx_s[...] = jnp.broadcast_to(v_cur, acc_fx_s.shape)
        carry_fx_s[...] = _lanes_to_col(row[:, gate_col:gate_col + LANES], n_fx)

    j = lax.broadcasted_iota(jnp.int32, (page, page), 0)
    s = lax.broadcasted_iota(jnp.int32, (page, page), 1)
    later_sel = (j > s).astype(F32)

    def lanes(refs, dtype):
        return jnp.concatenate([r[...].astype(dtype) for r in refs], axis=1)

    def later_bias(x, carry):
        parts = [x[:, g * page:(g + 1) * page] for g in range(group)]
        inside = jnp.dot(jnp.concatenate(parts, axis=0), later_sel, preferred_element_type=F32,
                         precision=lax.Precision.HIGHEST)
        rows = x.shape[0]
        out = []
        for g in range(group):
            out.append(inside[g * rows:(g + 1) * rows] + carry)
            carry = carry + jnp.sum(parts[g], axis=1, keepdims=True)
        return jnp.concatenate(out, axis=1), carry

    z = jnp.dot(qsb_s[...], lanes(ksb_refs, BF16), preferred_element_type=F32)
    log_beta = _log_sigmoid(z)
    later, carry_sb_s[...] = later_bias(log_beta - z, carry_sb_s[...])
    w = jnp.exp(log_beta + later)
    acc_sb_s[...] += lax.dot_general(w.astype(BF16), lanes(vsb_refs, BF16), NT_DIMS, preferred_element_type=F32)

    bias, carry_fx_s[...] = later_bias(lanes(lft_refs, F32), carry_fx_s[...])
    sc = jnp.dot(qfx_s[...], lanes(kfx_refs, BF16), preferred_element_type=F32) + bias
    m_new = jnp.maximum(m_s[...], jnp.max(sc, axis=1, keepdims=True))
    alpha = jnp.exp(m_s[...] - m_new)
    pr = jnp.exp(sc - m_new)
    l_s[...] = alpha * l_s[...] + jnp.sum(pr, axis=1, keepdims=True)
    acc_fx_s[...] = alpha * acc_fx_s[...] + lax.dot_general(
        pr.astype(BF16), lanes(vfx_refs, BF16), NT_DIMS, preferred_element_type=F32)
    m_s[...] = m_new

    @pl.when(p == pl.num_programs(1) - 1)
    def _():
        o_sb = jnp.sum(jnp.where(mask_sb, acc_sb_s[...], 0.0), axis=0, keepdims=True)
        o_fx = jnp.sum(jnp.where(mask_fx, acc_fx_s[...] / l_s[...], 0.0), axis=0, keepdims=True)
        o_ref[:, 0:w_sb] = o_sb
        o_ref[:, w_sb:w_sb + w_fx] = o_fx


def _decode_attn(proj, page_table, layer, cache_sb_k, cache_sb_v, cache_fox_k, cache_fox_v, cache_fox_logf, cols):
    b, n = proj.shape
    _, n_phys, page, n_sb, hd_sb = cache_sb_k.shape
    n_fx, hd_fx = cache_fox_k.shape[-2:]
    n_pages = page_table.shape[1]
    w_sb, w_fx = n_sb * hd_sb, n_fx * hd_fx
    flat = lambda a: jnp.transpose(a, (0, 1, 3, 4, 2)).reshape(a.shape[0], n_phys, -1, page)
    lft = jnp.swapaxes(cache_fox_logf, 2, 3)

    group = min(n_pages, DECODE_PAGES_PER_STEP)
    assert n_pages % group == 0

    def paged(rows, g):
        return pl.BlockSpec((None, None, rows, page),
                            lambda i, p, pt: (layer, pt[i, n_pages - 1 - p * group - g], 0, 0))

    caches = (flat(cache_sb_k), flat(cache_sb_v), flat(cache_fox_k), flat(cache_fox_v), lft)
    rows = (w_sb, w_sb, w_fx, w_fx, n_fx)
    grid_spec = pltpu.PrefetchScalarGridSpec(
        num_scalar_prefetch=1,
        grid=(b, n_pages // group),
        in_specs=[pl.BlockSpec((None, 1, n), lambda i, p, pt: (i, 0, 0))]
        + [paged(r, g) for r in rows for g in range(group)],
        out_specs=pl.BlockSpec((None, 1, w_sb + w_fx), lambda i, p, pt: (i, 0, 0)),
        scratch_shapes=[pltpu.VMEM((n_sb, w_sb), BF16), pltpu.VMEM((n_fx, w_fx), BF16),
                        pltpu.VMEM((n_sb, 1), F32), pltpu.VMEM((n_sb, w_sb), F32),
                        pltpu.VMEM((n_fx, 1), F32), pltpu.VMEM((n_fx, 1), F32), pltpu.VMEM((n_fx, 1), F32),
                        pltpu.VMEM((n_fx, w_fx), F32)])
    out = pl.pallas_call(
        functools.partial(_decode_attn_kernel, group=group, n_sb=n_sb, hd_sb=hd_sb, n_fx=n_fx, hd_fx=hd_fx,
                          cols=cols),
        grid_spec=grid_spec,
        out_shape=jax.ShapeDtypeStruct((b, 1, w_sb + w_fx), F32),
        compiler_params=_params(("parallel", "arbitrary")),
        name="decode_attn",
    )(page_table, proj.reshape(b, 1, n), *[c for c in caches for _ in range(group)])
    return out.reshape(b, w_sb + w_fx)


def _mlstm_step_kernel(row_ref, ng_ref, c0_ref, n0_ref, m0_ref, gated_ref, c_ref, n_ref, m_ref,
                       *, n_heads, dk, dv, gate_col):
    row = row_ref[...]
    gates = row[:, gate_col:gate_col + LANES]
    eye = (lax.broadcasted_iota(jnp.int32, (dv, dv), 0) == lax.broadcasted_iota(jnp.int32, (dv, dv), 1))
    for h in range(n_heads):
        q = row[:, h * dk:(h + 1) * dk]
        k = row[:, (n_heads + h) * dk:(n_heads + h + 1) * dk] * dk ** -0.5
        v = row[:, 2 * n_heads * dk + h * dv:2 * n_heads * dk + (h + 1) * dv]
        og = row[:, 2 * n_heads * dk + (n_heads + h) * dv:2 * n_heads * dk + (n_heads + h + 1) * dv]
        i_log = gates[:, h:h + 1]
        f_log = gates[:, n_heads + h:n_heads + h + 1]
        c_prev = c0_ref[h]
        n_prev = n0_ref[h:h + 1, :]
        m_prev = m0_ref[h:h + 1, :]
        g = f_log + m_prev
        m_t = jnp.maximum(g, i_log)
        w_in = jnp.exp(i_log - m_t)
        decay = jnp.exp(g - m_t)
        qk = jnp.sum(q * k, axis=1, keepdims=True)
        q8 = jnp.broadcast_to(q, (SUBLANES, dk)).astype(BF16)
        cq = lax.dot_general(q8, c_prev.astype(BF16), NT_DIMS, preferred_element_type=F32)[0:1, :]
        num = decay * cq + (w_in * qk) * v
        den = decay * jnp.sum(q * n_prev, axis=1, keepdims=True) + w_in * qk
        hid = num / jnp.maximum(jnp.abs(den), jnp.exp(-m_t))
        hn = _rms(hid, ng_ref[h:h + 1, :])
        gated_ref[:, h * dv:(h + 1) * dv] = _sigmoid(og) * hn
        v_col = jnp.sum(jnp.where(eye, v, 0.0), axis=1, keepdims=True)
        c_ref[h] = decay * c_prev + v_col * (w_in * k)
        n_ref[h:h + 1, :] = decay * n_prev + w_in * k
        m_ref[h:h + 1, :] = m_t


def _mlstm_decode(proj, norm_g, c0, n0, m0, gate_col):
    b, n = proj.shape
    _, n_heads, dv, dk = c0.shape
    seq_spec = lambda *tail: pl.BlockSpec((None,) + tail, lambda i: (i,) + (0,) * len(tail))
    gated, c, nn, m = pl.pallas_call(
        functools.partial(_mlstm_step_kernel, n_heads=n_heads, dk=dk, dv=dv, gate_col=gate_col),
        grid=(b,),
        in_specs=[seq_spec(1, n), _resident((n_heads, dv)), seq_spec(n_heads, dv, dk),
                  seq_spec(n_heads, dk), seq_spec(n_heads, 1)],
        out_specs=[seq_spec(1, n_heads * dv), seq_spec(n_heads, dv, dk), seq_spec(n_heads, dk),
                   seq_spec(n_heads, 1)],
        out_shape=[jax.ShapeDtypeStruct((b, 1, n_heads * dv), F32),
                   jax.ShapeDtypeStruct((b, n_heads, dv, dk), F32),
                   jax.ShapeDtypeStruct((b, n_heads, dk), F32),
                   jax.ShapeDtypeStruct((b, n_heads, 1), F32)],
        compiler_params=_params(("parallel",)),
        name="mlstm_step",
    )(proj.reshape(b, 1, n), norm_g, c0, n0, m0.reshape(b, n_heads, 1))
    return gated.reshape(b, n_heads * dv), c, nn, m.reshape(b, n_heads)


def _pad_gate_cols(w, n_main):
    n_gates = w.shape[1] - n_main
    return jnp.pad(w, ((0, 0), (0, LANES - n_gates))).astype(BF16)


def _gate_rows(bias, logsig_from):
    n = bias.shape[0]
    gb = jnp.pad(bias.astype(F32), (0, LANES - n)).reshape(1, LANES)
    gm = (jnp.arange(LANES) >= logsig_from) & (jnp.arange(LANES) < n)
    return gb, gm.astype(F32).reshape(1, LANES)


def _trunk(x, caches, page_table, ml_state, p):
    bsz, seq, d = x.shape
    m = bsz * seq
    x = x.reshape(m, d)
    depth = p["norm_g"].shape[0]
    n_sb, hd_sb, n_fx, hd_fx = p["n_sb"], p["hd_sb"], p["n_fx"], p["hd_fx"]
    w_sb, w_fx = n_sb * hd_sb, n_fx * hd_fx
    n_ml, dk, dv = p["n_ml"], p["dk"], p["dv"]
    att_rows, ml_rows = [], []
    for layer in range(depth):
        idx = layer // 2
        g = lambda j: p["norm_g"][layer, j].reshape(1, d)
        x = _ffn(x, g(0), p["ffn1_w_in"][layer], p["ffn1_w_out"][layer], g(1))
        if layer % 2 == 0:
            gb, gm = _gate_rows(p["att_b_f"][idx], 0)
            main = 3 * w_sb + 3 * w_fx
            kv_cols = (w_sb, 2 * w_sb, 3 * w_sb + w_fx, 3 * w_sb + 2 * w_fx)
            kv_heads = ((n_sb, hd_sb), (n_sb, hd_sb), (n_fx, hd_fx), (n_fx, hd_fx))
            if caches is None:
                proj, *kv_t = _proj(x, g(2), p["att_w_in"][idx], gb, gm, seq, kv_cols)
                k_sb, v_sb, k_fx, v_fx = (
                    jnp.transpose(t.reshape(bsz, nh, hd, seq), (0, 3, 1, 2)) for t, (nh, hd) in zip(kv_t, kv_heads))
            else:
                proj, = _proj(x, g(2), p["att_w_in"][idx], gb, gm)
                k_sb, v_sb, k_fx, v_fx = (
                    proj[:, c:c + nh * hd].reshape(bsz, seq, nh, hd) for c, (nh, hd) in zip(kv_cols, kv_heads))
            lf = proj[:, main:main + n_fx].reshape(bsz, seq, n_fx)
            att_rows.append((k_sb, v_sb, k_fx, v_fx, lf))
            w_out = p["att_w_out"][idx]
            if caches is None:
                cb = lambda off: off // LANES
                o_sb = _sb_prefill(proj, bsz, seq, cb(0), cb(w_sb), cb(2 * w_sb), w_sb // LANES, hd_sb)
                c = _cumsum_lanes(jnp.swapaxes(lf, 1, 2))
                o_fx = _fox_prefill(proj, c, bsz, seq, cb(3 * w_sb), cb(3 * w_sb + w_fx),
                                    cb(3 * w_sb + 2 * w_fx), w_fx // LANES, hd_fx)
                x = _outproj([o_sb, o_fx], [w_out[:w_sb], w_out[w_sb:]], g(3), x)
            else:
                assert seq == 1
                cols = (0, 3 * w_sb, 3 * w_sb + w_fx, 3 * w_sb + 2 * w_fx, main)
                o = _decode_attn(proj, page_table, idx, *caches, cols)
                x = _outproj([o], [w_out], g(3), x)
        else:
            gb, gm = _gate_rows(jnp.concatenate([p["ml_b_i"][idx], p["ml_b_f"][idx]]), n_ml)
            proj, = _proj(x, g(2), p["ml_w_in"][idx], gb, gm)
            main = 2 * n_ml * dk + 2 * n_ml * dv
            if ml_state is None:
                gates = proj[:, main:main + 2 * n_ml].reshape(bsz, seq, 2 * n_ml)
                gates = jnp.transpose(gates, (0, 2, 1))
                zeros = lambda *s: jnp.zeros(s, F32)
                gated, c_new, n_new, m_new = _mlstm_prefill(
                    proj, gates[:, :n_ml], gates[:, n_ml:], p["ml_norm_g"][idx],
                    zeros(bsz, n_ml, dv, dk), zeros(bsz, n_ml, dk), zeros(bsz, n_ml), bsz, seq, n_ml, dk, dv)
                n_new = n_new.reshape(bsz, n_ml, dk)
                m_new = m_new.reshape(bsz, n_ml)
            else:
                assert seq == 1
                gated, c_new, n_new, m_new = _mlstm_decode(
                    proj, p["ml_norm_g"][idx], ml_state[0][idx], ml_state[1][idx], ml_state[2][idx], main)
            ml_rows.append((c_new, n_new, m_new))
            x = _outproj([gated], [p["ml_w_out"][idx]], g(3), x)
        x = _ffn(x, g(4), p["ffn2_w_in"][layer], p["ffn2_w_out"][layer], g(5))
    att_new = [jnp.stack([r[i] for r in att_rows]) for i in range(5)]
    ml_new = [jnp.stack([r[i] for r in ml_rows]) for i in range(3)]
    return x.reshape(bsz, seq, d), att_new, ml_new


def kernel(x_prompt, x_sample, cache_sb_k, cache_sb_v, cache_fox_k, cache_fox_v, cache_fox_logf,
           state_mlstm_C, state_mlstm_n, state_mlstm_m, page_table, norm_g, ffn1_w_in, ffn1_w_out,
           ffn2_w_in, ffn2_w_out, att_w_in, att_b_f, att_w_out, ml_w_in, ml_b_i, ml_b_f, ml_norm_g,
           ml_w_out):
    n_sb, hd_sb = cache_sb_k.shape[-2:]
    n_fx, hd_fx = cache_fox_k.shape[-2:]
    n_ml, dv, dk = state_mlstm_C.shape[-3:]
    att_main = 3 * n_sb * hd_sb + 3 * n_fx * hd_fx
    ml_main = 2 * n_ml * dk + 2 * n_ml * dv
    p = dict(
        norm_g=norm_g, n_sb=n_sb, hd_sb=hd_sb, n_fx=n_fx, hd_fx=hd_fx, n_ml=n_ml, dk=dk, dv=dv,
        ffn1_w_in=ffn1_w_in.astype(BF16), ffn1_w_out=ffn1_w_out.astype(BF16),
        ffn2_w_in=ffn2_w_in.astype(BF16), ffn2_w_out=ffn2_w_out.astype(BF16),
        att_w_in=jnp.stack([_pad_gate_cols(w, att_main) for w in att_w_in]), att_b_f=att_b_f,
        att_w_out=att_w_out.astype(BF16),
        ml_w_in=jnp.stack([_pad_gate_cols(w, ml_main) for w in ml_w_in]), ml_b_i=ml_b_i, ml_b_f=ml_b_f,
        ml_norm_g=ml_norm_g, ml_w_out=ml_w_out.astype(BF16))
    y_p, att_p, ml_p = _trunk(x_prompt, None, None, None, p)
    caches = (cache_sb_k, cache_sb_v, cache_fox_k, cache_fox_v, cache_fox_logf)
    y_s, att_s, ml_s = _trunk(x_sample, caches, page_table,
                              (state_mlstm_C, state_mlstm_n, state_mlstm_m), p)
    return (y_p, y_s, *att_p, *ml_p, *att_s, *ml_s)
```

```python
import functools

import jax
import jax.numpy as jnp
from jax import lax
from jax.experimental import pallas as pl
from jax.experimental.pallas import tpu as pltpu

NORM_EPS = 1e-6
LANES = 128
SUBLANES = 8
VMEM_LIMIT_BYTES = 56 * 1024 * 1024
MXU_DIM = 256
SB_TQ = 1024
FOX_TQ = 512
FOX_TK = 512
MLSTM_CHUNK = 256
MASKED_MAX = -1e30
DECODE_PAGES_PER_STEP = 16

F32 = jnp.float32
BF16 = jnp.bfloat16
NT_DIMS = (((1,), (1,)), ((), ()))


def _rms(x, g):
    return x * lax.rsqrt(jnp.mean(x * x, axis=-1, keepdims=True) + NORM_EPS) * g


def _log_sigmoid(x):
    return jnp.minimum(x, 0.0) - jnp.log(1.0 + jnp.exp(-jnp.abs(x)))


def _sigmoid(x):
    return 1.0 / (1.0 + jnp.exp(-x))


def _resident(shape):
    return pl.BlockSpec(shape, lambda *_: (0,) * len(shape), pipeline_mode=pl.Buffered(1))


def _params(semantics):
    return pltpu.CompilerParams(dimension_semantics=semantics, vmem_limit_bytes=VMEM_LIMIT_BYTES)


def _ffn_kernel(x_ref, gpre_ref, win_ref, wout_ref, gpost_ref, o_ref, acc_ref, *, d_ff, tf):
    x = x_ref[...]
    h = _rms(x, gpre_ref[...]).astype(BF16)
    for c in range(d_ff // tf):
        gate = jnp.dot(h, win_ref[:, c * tf:(c + 1) * tf], preferred_element_type=F32)
        up = jnp.dot(h, win_ref[:, d_ff + c * tf:d_ff + (c + 1) * tf], preferred_element_type=F32)
        act = (gate * _sigmoid(gate) * up).astype(BF16)
        part = jnp.dot(act, wout_ref[c * tf:(c + 1) * tf, :], preferred_element_type=F32)
        if c == 0:
            acc_ref[...] = part
        else:
            acc_ref[...] += part
    o_ref[...] = x + 0.5 * _rms(acc_ref[...], gpost_ref[...])


def _ffn(x, g_pre, w_in, w_out, g_post):
    m, d = x.shape
    d_ff = w_out.shape[0]
    tm = min(m, 512)
    tf = 256
    assert m % tm == 0 and d_ff % tf == 0
    return pl.pallas_call(
        functools.partial(_ffn_kernel, d_ff=d_ff, tf=tf),
        grid=(m // tm,),
        in_specs=[pl.BlockSpec((tm, d), lambda i: (i, 0)),
                  _resident((1, d)), _resident((d, 2 * d_ff)), _resident((d_ff, d)), _resident((1, d))],
        out_specs=pl.BlockSpec((tm, d), lambda i: (i, 0)),
        out_shape=jax.ShapeDtypeStruct((m, d), F32),
        scratch_shapes=[pltpu.VMEM((tm, d), F32)],
        compiler_params=_params(("parallel",)),
        name="ffn",
    )(x, g_pre, w_in, w_out, g_post)


def _proj_kernel(x_ref, g_ref, w_ref, gbias_ref, glogsig_ref, o_ref, *t_refs, n_main, chunk, t_cols):
    h = _rms(x_ref[...], g_ref[...]).astype(BF16)
    for c0 in range(0, n_main, chunk):
        c1 = min(c0 + chunk, n_main)
        y = jnp.dot(h, w_ref[:, c0:c1], preferred_element_type=F32)
        o_ref[:, c0:c1] = y
        if c0 in t_cols:
            t_refs[t_cols.index(c0)][...] = y.T
    y = jnp.dot(h, w_ref[:, n_main:n_main + LANES], preferred_element_type=F32) + gbias_ref[...]
    o_ref[:, n_main:n_main + LANES] = jnp.where(glogsig_ref[...] > 0.0, _log_sigmoid(y), y)


def _proj(x, g, w, gate_bias, gate_logsig, seq=None, t_cols=()):
    m, d = x.shape
    n = w.shape[1]
    n_main = n - LANES
    chunk = 512
    tm = min(m, 512)
    assert m % tm == 0 and all(c % chunk == 0 for c in t_cols)
    out_specs = [pl.BlockSpec((tm, n), lambda i: (i, 0))]
    out_shape = [jax.ShapeDtypeStruct((m, n), F32)]
    if t_cols:
        per_seq = seq // tm
        assert seq % tm == 0
        out_specs += [pl.BlockSpec((None, chunk, tm), lambda i: (i // per_seq, 0, i % per_seq))] * len(t_cols)
        out_shape += [jax.ShapeDtypeStruct((m // seq, chunk, seq), F32)] * len(t_cols)
    return pl.pallas_call(
        functools.partial(_proj_kernel, n_main=n_main, chunk=chunk, t_cols=tuple(t_cols)),
        grid=(m // tm,),
        in_specs=[pl.BlockSpec((tm, d), lambda i: (i, 0)),
                  _resident((1, d)), _resident((d, n)), _resident((1, LANES)), _resident((1, LANES))],
        out_specs=out_specs,
        out_shape=out_shape,
        compiler_params=_params(("parallel",)),
        name="proj",
    )(x, g, w, gate_bias, gate_logsig)


def _outproj_kernel(*refs, n_in):
    a_refs, w_refs = refs[:n_in], refs[n_in:2 * n_in]
    g_ref, x_ref, o_ref = refs[2 * n_in:]
    mix = None
    for a_ref, w_ref in zip(a_refs, w_refs):
        part = jnp.dot(a_ref[...].astype(BF16), w_ref[...], preferred_element_type=F32)
        mix = part if mix is None else mix + part
    o_ref[...] = x_ref[...] + _rms(mix, g_ref[...])


def _outproj(acts, ws, g, x):
    m, d = x.shape
    tm = min(m, 512)
    assert m % tm == 0
    n_in = len(acts)
    in_specs = [pl.BlockSpec((tm, a.shape[1]), lambda i: (i, 0)) for a in acts]
    in_specs += [_resident(w.shape) for w in ws]
    in_specs += [_resident((1, d)), pl.BlockSpec((tm, d), lambda i: (i, 0))]
    return pl.pallas_call(
        functools.partial(_outproj_kernel, n_in=n_in),
        grid=(m // tm,),
        in_specs=in_specs,
        out_specs=pl.BlockSpec((tm, d), lambda i: (i, 0)),
        out_shape=jax.ShapeDtypeStruct((m, d), F32),
        compiler_params=_params(("parallel",)),
        name="outproj",
    )(*acts, *ws, g, x)


def _head_lane_mask(hd, h):
    lane = lax.broadcasted_iota(jnp.int32, (1, LANES), 1)
    return (lane >= h * hd) & (lane < (h + 1) * hd)


def _sb_prefill_kernel(q_ref, k_ref, v_ref, sel_ref, o_ref, kb_s, vh_s, *, tq, tk, hd, scale):
    i = pl.program_id(2)
    heads = LANES // hd
    per_tile = tq // tk
    seq = k_ref.shape[0]

    @pl.when(i == 0)
    def _():
        def convert(c, carry):
            rows = pl.ds(pl.multiple_of(c * tk, tk), tk)
            kb_s[rows, :] = k_ref[rows, :].astype(BF16)
            v = v_ref[rows, :]
            for h in range(heads):
                vh_s[h, rows, :] = jnp.where(_head_lane_mask(hd, h), v, 0.0).astype(BF16)
            return carry
        lax.fori_loop(0, seq // tk, convert, 0)

    q = q_ref[...] * scale
    qh = [jnp.where(_head_lane_mask(hd, h), q, 0.0).astype(BF16) for h in range(heads)]

    def block(j, r0, carries, acc, diagonal):
        n = tq - r0
        keys = pl.ds(pl.multiple_of(j * tk, tk), tk)
        kb = kb_s[keys, :]
        if diagonal:
            causal = (lax.broadcasted_iota(jnp.int32, (n, tk), 1)
                      < lax.broadcasted_iota(jnp.int32, (n, tk), 0))
        new_carries = []
        part = None
        for h in range(heads):
            carry = carries[h][r0:]
            z = lax.dot_general(qh[h][r0:], kb, NT_DIMS, preferred_element_type=F32)
            log_beta = _log_sigmoid(z)
            log_1m = log_beta - z
            if diagonal:
                log_1m = jnp.where(causal, log_1m, 0.0)
            hi = log_1m.astype(BF16)
            lo = (log_1m - hi.astype(F32)).astype(BF16)
            both = jnp.dot(jnp.concatenate([hi, lo], axis=0), sel_ref[...], preferred_element_type=F32)
            later = both[:n] + both[n:]
            w = jnp.exp(log_beta + later)
            if diagonal:
                w = jnp.where(causal, w, 0.0)
            pv = jnp.exp(carry) * jnp.dot(w.astype(BF16), vh_s[h, keys, :], preferred_element_type=F32)
            part = pv if part is None else part + pv
            carry = carry + (later[:, 0:1] + log_1m[:, 0:1])
            new_carries.append(carry if r0 == 0 else jnp.concatenate([carries[h][:r0], carry], axis=0))
        acc = acc + part if r0 == 0 else jnp.concatenate([acc[:r0], acc[r0:] + part], axis=0)
        return tuple(new_carries), acc

    carries = tuple(jnp.zeros((tq, 1), F32) for _ in range(heads))
    acc = jnp.zeros((tq, LANES), F32)
    for d in reversed(range(per_tile)):
        carries, acc = block(i * per_tile + d, d * tk, carries, acc, True)

    def body(t, state):
        for u in range(per_tile):
            state = block((i - t) * per_tile - 1 - u, 0, state[0], state[1], False)
        return state

    carries, acc = lax.fori_loop(0, i, body, (carries, acc))
    o_ref[...] = acc


def _sb_prefill(proj, batch, seq, q_col, k_col, v_col, n_groups, hd):
    tq = min(seq, SB_TQ)
    tk = min(seq, MXU_DIM)
    nq = seq // tq
    assert seq % tq == 0 and tq % tk == 0
    later_sel = jnp.tril(jnp.ones((tk, tk), BF16), -1)
    return pl.pallas_call(
        functools.partial(_sb_prefill_kernel, tq=tq, tk=tk, hd=hd, scale=hd ** -0.5),
        grid=(batch, n_groups, nq),
        in_specs=[pl.BlockSpec((tq, LANES), lambda b, p, i: (b * nq + i, q_col + p)),
                  pl.BlockSpec((seq, LANES), lambda b, p, i: (b, k_col + p)),
                  pl.BlockSpec((seq, LANES), lambda b, p, i: (b, v_col + p)),
                  _resident((tk, tk))],
        out_specs=pl.BlockSpec((tq, LANES), lambda b, p, i: (b * nq + i, p)),
        out_shape=jax.ShapeDtypeStruct((batch * seq, n_groups * LANES), F32),
        scratch_shapes=[pltpu.VMEM((seq, LANES), BF16), pltpu.VMEM((LANES // hd, seq, LANES), BF16)],
        compiler_params=_params(("parallel", "parallel", "arbitrary")),
        name="sb_prefill",
    )(proj, proj, proj, later_sel)


def _cumsum_kernel(x_ref, o_ref, *, chunk):
    rows, t = x_ref.shape
    j = lax.broadcasted_iota(jnp.int32, (chunk, chunk), 0)
    s = lax.broadcasted_iota(jnp.int32, (chunk, chunk), 1)
    incl = (j <= s).astype(F32)
    carry = jnp.zeros((rows, 1), F32)
    for c in range(t // chunk):
        x = x_ref[:, c * chunk:(c + 1) * chunk]
        o_ref[:, c * chunk:(c + 1) * chunk] = carry + jnp.dot(
            x, incl, preferred_element_type=F32, precision=lax.Precision.HIGHEST)
        carry = carry + jnp.sum(x, axis=-1, keepdims=True)


def _cumsum_lanes(x):
    b, r, t = x.shape
    chunk = min(t, MXU_DIM)
    assert t % chunk == 0
    return pl.pallas_call(
        functools.partial(_cumsum_kernel, chunk=chunk),
        grid=(b,),
        in_specs=[pl.BlockSpec((None, r, t), lambda i: (i, 0, 0))],
        out_specs=pl.BlockSpec((None, r, t), lambda i: (i, 0, 0)),
        out_shape=jax.ShapeDtypeStruct((b, r, t), F32),
        compiler_params=_params(("parallel",)),
        name="cumsum",
    )(x)


def _fox_prefill_kernel(q_ref, k_ref, v_ref, cq_ref, ck_ref, o_ref, kb_s, vh_s, *, tq, tk, hd, scale):
    i = pl.program_id(2)
    heads = LANES // hd
    per_tile = tq // tk
    seq = k_ref.shape[0]
    lane = lax.broadcasted_iota(jnp.int32, (1, LANES), 1)
    ones_lane = [((h + 1) % heads) * hd for h in range(heads)]

    @pl.when(i == 0)
    def _():
        def convert(c, carry):
            rows = pl.ds(pl.multiple_of(c * tk, tk), tk)
            kb_s[rows, :] = k_ref[rows, :].astype(BF16)
            v = v_ref[rows, :]
            for h in range(heads):
                vh = jnp.where(_head_lane_mask(hd, h), v, jnp.where(lane == ones_lane[h], 1.0, 0.0))
                vh_s[h, rows, :] = vh.astype(BF16)
            return carry
        lax.fori_loop(0, seq // tk, convert, 0)

    q = q_ref[...] * scale
    qh = [jnp.where(_head_lane_mask(hd, h), q, 0.0).astype(BF16) for h in range(heads)]
    cq = [cq_ref[:, h:h + 1] for h in range(heads)]

    def block(j, r0, ms, accs, diagonal):
        n = tq - r0
        start = pl.multiple_of(j * tk, tk)
        keys = pl.ds(start, tk)
        kb = kb_s[keys, :]
        if diagonal:
            causal = (lax.broadcasted_iota(jnp.int32, (n, tk), 1)
                      <= lax.broadcasted_iota(jnp.int32, (n, tk), 0))
        new_ms, new_accs = [], []
        for h in range(heads):
            s = (lax.dot_general(qh[h][r0:], kb, NT_DIMS, preferred_element_type=F32)
                 + cq[h][r0:] - ck_ref[h:h + 1, keys])
            if diagonal:
                s = jnp.where(causal, s, -jnp.inf)
            m_old = ms[h][r0:]
            m_new = jnp.maximum(m_old, jnp.max(s, axis=-1, keepdims=True))
            p = jnp.exp(s - m_new)
            acc = (jnp.exp(m_old - m_new) * accs[h][r0:]
                   + jnp.dot(p.astype(BF16), vh_s[h, keys, :], preferred_element_type=F32))
            if r0:
                m_new = jnp.concatenate([ms[h][:r0], m_new], axis=0)
                acc = jnp.concatenate([accs[h][:r0], acc], axis=0)
            new_ms.append(m_new)
            new_accs.append(acc)
        return tuple(new_ms), tuple(new_accs)

    ms = tuple(jnp.full((tq, 1), MASKED_MAX, F32) for _ in range(heads))
    accs = tuple(jnp.zeros((tq, LANES), F32) for _ in range(heads))
    for d in reversed(range(per_tile)):
        ms, accs = block(i * per_tile + d, d * tk, ms, accs, True)

    past = i * per_tile

    def single(t, state):
        return block(past - 1 - t, 0, state[0], state[1], False)

    def pair(t, state):
        state = block(past - past % 2 - 1 - 2 * t, 0, state[0], state[1], False)
        return block(past - past % 2 - 2 - 2 * t, 0, state[0], state[1], False)

    ms, accs = lax.fori_loop(0, past % 2, single, (ms, accs))
    ms, accs = lax.fori_loop(0, past // 2, pair, (ms, accs))
    out = None
    for h in range(heads):
        denom = accs[h][:, ones_lane[h]:ones_lane[h] + 1]
        part = jnp.where(_head_lane_mask(hd, h), accs[h] / denom, 0.0)
        out = part if out is None else out + part
    o_ref[...] = out


def _fox_prefill(proj, c, batch, seq, q_col, k_col, v_col, n_groups, hd):
    tq = min(seq, FOX_TQ)
    tk = min(seq, FOX_TK)
    nq = seq // tq
    heads = LANES // hd
    assert seq % tq == 0 and tq % tk == 0 and heads >= 2
    ck = c.reshape(batch, n_groups, heads, seq)
    cq = jnp.swapaxes(ck, 2, 3)
    return pl.pallas_call(
        functools.partial(_fox_prefill_kernel, tq=tq, tk=tk, hd=hd, scale=hd ** -0.5),
        grid=(batch, n_groups, nq),
        in_specs=[pl.BlockSpec((tq, LANES), lambda b, p, i: (b * nq + i, q_col + p)),
                  pl.BlockSpec((seq, LANES), lambda b, p, i: (b, k_col + p)),
                  pl.BlockSpec((seq, LANES), lambda b, p, i: (b, v_col + p)),
                  pl.BlockSpec((None, None, tq, heads), lambda b, p, i: (b, p, i, 0)),
                  pl.BlockSpec((None, None, heads, seq), lambda b, p, i: (b, p, 0, 0))],
        out_specs=pl.BlockSpec((tq, LANES), lambda b, p, i: (b * nq + i, p)),
        out_shape=jax.ShapeDtypeStruct((batch * seq, n_groups * LANES), F32),
        scratch_shapes=[pltpu.VMEM((seq, LANES), BF16), pltpu.VMEM((heads, seq, LANES), BF16)],
        compiler_params=_params(("parallel", "parallel", "arbitrary")),
        name="fox_prefill",
    )(proj, proj, proj, cq, ck)


def _mlstm_chunk_kernel(q_ref, k_ref, v_ref, og_ref, icol_ref, fcol_ref, irow_ref, frow_ref, ng_ref,
                        c0_ref, n0_ref, m0_ref, gated_ref, c_ref, n_ref, m_ref, *, chunk, n_heads, dk, dv):
    step = pl.program_id(1)

    @pl.when(step == 0)
    def _():
        c_ref[...] = c0_ref[...]
        n_ref[...] = n0_ref[...]
        m_ref[...] = m0_ref[...]

    t_idx = lax.broadcasted_iota(jnp.int32, (chunk, chunk), 0)
    s_idx = lax.broadcasted_iota(jnp.int32, (chunk, chunk), 1)
    causal = s_idx <= t_idx
    for h in range(n_heads):
        q = q_ref[:, h * dk:(h + 1) * dk]
        k = k_ref[:, h * dk:(h + 1) * dk] * dk ** -0.5
        v = v_ref[:, h * dv:(h + 1) * dv]
        qb, kb = q.astype(BF16), k.astype(BF16)
        b_col = jnp.sum(jnp.where(causal, frow_ref[h], 0.0), axis=1, keepdims=True)
        b_row = jnp.sum(jnp.where(t_idx <= s_idx, fcol_ref[h], 0.0), axis=0, keepdims=True)
        m_prev = m_ref[h]
        n_prev = n_ref[h]
        c_prev = c_ref[h]

        d = jnp.where(causal, b_col - b_row + irow_ref[h], -jnp.inf)
        g = b_col + m_prev
        m_t = jnp.maximum(g, jnp.max(d, axis=1, keepdims=True))
        wt = jnp.exp(d - m_t) * lax.dot_general(qb, kb, NT_DIMS, preferred_element_type=F32)
        decay = jnp.exp(g - m_t)
        num = (decay * lax.dot_general(qb, c_prev.astype(BF16), NT_DIMS, preferred_element_type=F32)
               + jnp.dot(wt.astype(BF16), v.astype(BF16), preferred_element_type=F32))
        den = decay * jnp.sum(q * n_prev, axis=1, keepdims=True) + jnp.sum(wt, axis=1, keepdims=True)
        hid = num / jnp.maximum(jnp.abs(den), jnp.exp(-m_t))

        hn = _rms(hid, ng_ref[h])
        gated_ref[:, h * dv:(h + 1) * dv] = (_sigmoid(og_ref[:, h * dv:(h + 1) * dv]) * hn).astype(gated_ref.dtype)

        b_last = b_col[chunk - 1:chunk, :]
        a_col = b_last - b_col + icol_ref[h]
        m_new = jnp.maximum(b_last + m_prev, jnp.max(a_col, axis=0, keepdims=True))
        w_col = jnp.exp(a_col - m_new)
        carry_decay = jnp.exp(b_last + m_prev - m_new)
        wv_t = (w_col * v).T.astype(BF16)
        c_ref[h] = carry_decay * c_prev + jnp.dot(wv_t, kb, preferred_element_type=F32)
        n_ref[h] = carry_decay * n_prev + jnp.sum(w_col * k, axis=0, keepdims=True)
        m_ref[h] = m_new


def _mlstm_prefill(proj, gates_i, gates_f, norm_g, c0, n0, m0, batch, seq, n_heads, dk, dv):
    chunk = min(seq, MLSTM_CHUNK)
    nc = seq // chunk
    assert seq % chunk == 0 and dk == dv
    row4 = lambda a: a.reshape(batch, n_heads, 1, seq)
    col4 = lambda a: a.reshape(batch, n_heads, seq, 1)
    tok = lambda group: pl.BlockSpec((chunk, n_heads * dk), lambda b, c: (b * nc + c, group))
    colspec = pl.BlockSpec((None, n_heads, chunk, 1), lambda b, c: (b, 0, c, 0))
    rowspec = pl.BlockSpec((None, n_heads, 1, chunk), lambda b, c: (b, 0, 0, c))
    cspec = pl.BlockSpec((None, n_heads, dv, dk), lambda b, c: (b, 0, 0, 0))
    nspec = pl.BlockSpec((None, n_heads, 1, dk), lambda b, c: (b, 0, 0, 0))
    mspec = pl.BlockSpec((None, n_heads, 1, 1), lambda b, c: (b, 0, 0, 0))
    return pl.pallas_call(
        functools.partial(_mlstm_chunk_kernel, chunk=chunk, n_heads=n_heads, dk=dk, dv=dv),
        grid=(batch, nc),
        in_specs=[tok(0), tok(1), tok(2), tok(3), colspec, colspec, rowspec, rowspec,
                  _resident((n_heads, 1, dv)), cspec, nspec, mspec],
        out_specs=[pl.BlockSpec((chunk, n_heads * dv), lambda b, c: (b * nc + c, 0)), cspec, nspec, mspec],
        out_shape=[jax.ShapeDtypeStruct((batch * seq, n_heads * dv), BF16),
                   jax.ShapeDtypeStruct((batch, n_heads, dv, dk), F32),
                   jax.ShapeDtypeStruct((batch, n_heads, 1, dk), F32),
                   jax.ShapeDtypeStruct((batch, n_heads, 1, 1), F32)],
        compiler_params=_params(("parallel", "arbitrary")),
        name="mlstm_chunk",
    )(proj, proj, proj, proj, col4(gates_i), col4(gates_f), row4(gates_i), row4(gates_f),
      norm_g.reshape(n_heads, 1, dv), c0, n0.reshape(batch, n_heads, 1, dk), m0.reshape(batch, n_heads, 1, 1))


def _block_diag_mask(n_heads, hd):
    r = lax.broadcasted_iota(jnp.int32, (n_heads, n_heads * hd), 0)
    lane = lax.broadcasted_iota(jnp.int32, (n_heads, n_heads * hd), 1)
    return (lane >= r * hd) & (lane < (r + 1) * hd)


def _lanes_to_col(row, n):
    r = lax.broadcasted_iota(jnp.int32, (n, row.shape[1]), 0)
    lane = lax.broadcasted_iota(jnp.int32, (n, row.shape[1]), 1)
    return jnp.sum(jnp.where(r == lane, row, 0.0), axis=1, keepdims=True)


def _decode_attn_kernel(pt_ref, row_ref, *refs, group, n_sb, hd_sb, n_fx, hd_fx, cols):
    del pt_ref
    ksb_refs, vsb_refs, kfx_refs, vfx_refs, lft_refs = (refs[g * group:(g + 1) * group] for g in range(5))
    o_ref, qsb_s, qfx_s, carry_sb_s, acc_sb_s, carry_fx_s, m_s, l_s, acc_fx_s = refs[5 * group:]
    p = pl.program_id(1)
    page = ksb_refs[0].shape[1]
    w_sb, w_fx = n_sb * hd_sb, n_fx * hd_fx
    q_sb_col, q_fx_col, k_fx_col, v_fx_col, gate_col = cols
    mask_sb = _block_diag_mask(n_sb, hd_sb)
    mask_fx = _block_diag_mask(n_fx, hd_fx)

    @pl.when(p == 0)
    def _():
        row = row_ref[...]
        q_sb = jnp.where(mask_sb, row[:, q_sb_col:q_sb_col + w_sb] * hd_sb ** -0.5, 0.0)
        q_fx = jnp.where(mask_fx, row[:, q_fx_col:q_fx_col + w_fx] * hd_fx ** -0.5, 0.0)
        qsb_s[...] = q_sb.astype(BF16)
        qfx_s[...] = q_fx.astype(BF16)
        carry_sb_s[...] = jnp.zeros_like(carry_sb_s)
        acc_sb_s[...] = jnp.zeros_like(acc_sb_s)
        k_cur = row[:, k_fx_col:k_fx_col + w_fx]
        v_cur = row[:, v_fx_col:v_fx_col + w_fx]
        m_s[...] = jnp.sum(q_fx * k_cur, axis=1, keepdims=True)
        l_s[...] = jnp.ones_like(l_s)
        acc_fx_s[...] = jnp.broadcast_to(v_cur, acc_fx_s.shape)
        carry_fx_s[...] = _lanes_to_col(row[:, gate_col:gate_col + LANES], n_fx)

    j = lax.broadcasted_iota(jnp.int32, (page, page), 0)
    s = lax.broadcasted_iota(jnp.int32, (page, page), 1)
    later_sel = (j > s).astype(F32)

    def lanes(refs, dtype):
        return jnp.concatenate([r[...].astype(dtype) for r in refs], axis=1)

    def later_bias(x, carry):
        parts = [x[:, g * page:(g + 1) * page] for g in range(group)]
        inside = jnp.dot(jnp.concatenate(parts, axis=0), later_sel, preferred_element_type=F32,
                         precision=lax.Precision.HIGHEST)
        rows = x.shape[0]
        out = []
        for g in range(group):
            out.append(inside[g * rows:(g + 1) * rows] + carry)
            carry = carry + jnp.sum(parts[g], axis=1, keepdims=True)
        return jnp.concatenate(out, axis=1), carry

    z = jnp.dot(qsb_s[...], lanes(ksb_refs, BF16), preferred_element_type=F32)
    log_beta = _log_sigmoid(z)
    later, carry_sb_s[...] = later_bias(log_beta - z, carry_sb_s[...])
    w = jnp.exp(log_beta + later)
    acc_sb_s[...] += lax.dot_general(w.astype(BF16), lanes(vsb_refs, BF16), NT_DIMS, preferred_element_type=F32)

    bias, carry_fx_s[...] = later_bias(lanes(lft_refs, F32), carry_fx_s[...])
    sc = jnp.dot(qfx_s[...], lanes(kfx_refs, BF16), preferred_element_type=F32) + bias
    m_new = jnp.maximum(m_s[...], jnp.max(sc, axis=1, keepdims=True))
    alpha = jnp.exp(m_s[...] - m_new)
    pr = jnp.exp(sc - m_new)
    l_s[...] = alpha * l_s[...] + jnp.sum(pr, axis=1, keepdims=True)
    acc_fx_s[...] = alpha * acc_fx_s[...] + lax.dot_general(
        pr.astype(BF16), lanes(vfx_refs, BF16), NT_DIMS, preferred_element_type=F32)
    m_s[...] = m_new

    @pl.when(p == pl.num_programs(1) - 1)
    def _():
        o_sb = jnp.sum(jnp.where(mask_sb, acc_sb_s[...], 0.0), axis=0, keepdims=True)
        o_fx = jnp.sum(jnp.where(mask_fx, acc_fx_s[...] / l_s[...], 0.0), axis=0, keepdims=True)
        o_ref[:, 0:w_sb] = o_sb
        o_ref[:, w_sb:w_sb + w_fx] = o_fx


def _decode_attn(proj, page_table, layer, cache_sb_k, cache_sb_v, cache_fox_k, cache_fox_v, cache_fox_logf, cols):
    b, n = proj.shape
    _, n_phys, page, n_sb, hd_sb = cache_sb_k.shape
    n_fx, hd_fx = cache_fox_k.shape[-2:]
    n_pages = page_table.shape[1]
    w_sb, w_fx = n_sb * hd_sb, n_fx * hd_fx
    flat = lambda a: jnp.transpose(a, (0, 1, 3, 4, 2)).reshape(a.shape[0], n_phys, -1, page)
    lft = jnp.swapaxes(cache_fox_logf, 2, 3)

    group = min(n_pages, DECODE_PAGES_PER_STEP)
    assert n_pages % group == 0

    def paged(rows, g):
        return pl.BlockSpec((None, None, rows, page),
                            lambda i, p, pt: (layer, pt[i, n_pages - 1 - p * group - g], 0, 0))

    caches = (flat(cache_sb_k), flat(cache_sb_v), flat(cache_fox_k), flat(cache_fox_v), lft)
    rows = (w_sb, w_sb, w_fx, w_fx, n_fx)
    grid_spec = pltpu.PrefetchScalarGridSpec(
        num_scalar_prefetch=1,
        grid=(b, n_pages // group),
        in_specs=[pl.BlockSpec((None, 1, n), lambda i, p, pt: (i, 0, 0))]
        + [paged(r, g) for r in rows for g in range(group)],
        out_specs=pl.BlockSpec((None, 1, w_sb + w_fx), lambda i, p, pt: (i, 0, 0)),
        scratch_shapes=[pltpu.VMEM((n_sb, w_sb), BF16), pltpu.VMEM((n_fx, w_fx), BF16),
                        pltpu.VMEM((n_sb, 1), F32), pltpu.VMEM((n_sb, w_sb), F32),
                        pltpu.VMEM((n_fx, 1), F32), pltpu.VMEM((n_fx, 1), F32), pltpu.VMEM((n_fx, 1), F32),
                        pltpu.VMEM((n_fx, w_fx), F32)])
    out = pl.pallas_call(
        functools.partial(_decode_attn_kernel, group=group, n_sb=n_sb, hd_sb=hd_sb, n_fx=n_fx, hd_fx=hd_fx,
                          cols=cols),
        grid_spec=grid_spec,
        out_shape=jax.ShapeDtypeStruct((b, 1, w_sb + w_fx), F32),
        compiler_params=_params(("parallel", "arbitrary")),
        name="decode_attn",
    )(page_table, proj.reshape(b, 1, n), *[c for c in caches for _ in range(group)])
    return out.reshape(b, w_sb + w_fx)


def _mlstm_step_kernel(row_ref, ng_ref, c0_ref, n0_ref, m0_ref, gated_ref, c_ref, n_ref, m_ref,
                       *, n_heads, dk, dv, gate_col):
    row = row_ref[...]
    gates = row[:, gate_col:gate_col + LANES]
    eye = (lax.broadcasted_iota(jnp.int32, (dv, dv), 0) == lax.broadcasted_iota(jnp.int32, (dv, dv), 1))
    for h in range(n_heads):
        q = row[:, h * dk:(h + 1) * dk]
        k = row[:, (n_heads + h) * dk:(n_heads + h + 1) * dk] * dk ** -0.5
        v = row[:, 2 * n_heads * dk + h * dv:2 * n_heads * dk + (h + 1) * dv]
        og = row[:, 2 * n_heads * dk + (n_heads + h) * dv:2 * n_heads * dk + (n_heads + h + 1) * dv]
        i_log = gates[:, h:h + 1]
        f_log = gates[:, n_heads + h:n_heads + h + 1]
        c_prev = c0_ref[h]
        n_prev = n0_ref[h:h + 1, :]
        m_prev = m0_ref[h:h + 1, :]
        g = f_log + m_prev
        m_t = jnp.maximum(g, i_log)
        w_in = jnp.exp(i_log - m_t)
        decay = jnp.exp(g - m_t)
        qk = jnp.sum(q * k, axis=1, keepdims=True)
        q8 = jnp.broadcast_to(q, (SUBLANES, dk)).astype(BF16)
        cq = lax.dot_general(q8, c_prev.astype(BF16), NT_DIMS, preferred_element_type=F32)[0:1, :]
        num = decay * cq + (w_in * qk) * v
        den = decay * jnp.sum(q * n_prev, axis=1, keepdims=True) + w_in * qk
        hid = num / jnp.maximum(jnp.abs(den), jnp.exp(-m_t))
        hn = _rms(hid, ng_ref[h:h + 1, :])
        gated_ref[:, h * dv:(h + 1) * dv] = _sigmoid(og) * hn
        v_col = jnp.sum(jnp.where(eye, v, 0.0), axis=1, keepdims=True)
        c_ref[h] = decay * c_prev + v_col * (w_in * k)
        n_ref[h:h + 1, :] = decay * n_prev + w_in * k
        m_ref[h:h + 1, :] = m_t


def _mlstm_decode(proj, norm_g, c0, n0, m0, gate_col):
    b, n = proj.shape
    _, n_heads, dv, dk = c0.shape
    seq_spec = lambda *tail: pl.BlockSpec((None,) + tail, lambda i: (i,) + (0,) * len(tail))
    gated, c, nn, m = pl.pallas_call(
        functools.partial(_mlstm_step_kernel, n_heads=n_heads, dk=dk, dv=dv, gate_col=gate_col),
        grid=(b,),
        in_specs=[seq_spec(1, n), _resident((n_heads, dv)), seq_spec(n_heads, dv, dk),
                  seq_spec(n_heads, dk), seq_spec(n_heads, 1)],
        out_specs=[seq_spec(1, n_heads * dv), seq_spec(n_heads, dv, dk), seq_spec(n_heads, dk),
                   seq_spec(n_heads, 1)],
        out_shape=[jax.ShapeDtypeStruct((b, 1, n_heads * dv), F32),
                   jax.ShapeDtypeStruct((b, n_heads, dv, dk), F32),
                   jax.ShapeDtypeStruct((b, n_heads, dk), F32),
                   jax.ShapeDtypeStruct((b, n_heads, 1), F32)],
        compiler_params=_params(("parallel",)),
        name="mlstm_step",
    )(proj.reshape(b, 1, n), norm_g, c0, n0, m0.reshape(b, n_heads, 1))
    return gated.reshape(b, n_heads * dv), c, nn, m.reshape(b, n_heads)


def _pad_gate_cols(w, n_main):
    n_gates = w.shape[1] - n_main
    return jnp.pad(w, ((0, 0), (0, LANES - n_gates))).astype(BF16)


def _gate_rows(bias, logsig_from):
    n = bias.shape[0]
    gb = jnp.pad(bias.astype(F32), (0, LANES - n)).reshape(1, LANES)
    gm = (jnp.arange(LANES) >= logsig_from) & (jnp.arange(LANES) < n)
    return gb, gm.astype(F32).reshape(1, LANES)


def _trunk(x, caches, page_table, ml_state, p):
    bsz, seq, d = x.shape
    m = bsz * seq
    x = x.reshape(m, d)
    depth = p["norm_g"].shape[0]
    n_sb, hd_sb, n_fx, hd_fx = p["n_sb"], p["hd_sb"], p["n_fx"], p["hd_fx"]
    w_sb, w_fx = n_sb * hd_sb, n_fx * hd_fx
    n_ml, dk, dv = p["n_ml"], p["dk"], p["dv"]
    att_rows, ml_rows = [], []
    for layer in range(depth):
        idx = layer // 2
        g = lambda j: p["norm_g"][layer, j].reshape(1, d)
        x = _ffn(x, g(0), p["ffn1_w_in"][layer], p["ffn1_w_out"][layer], g(1))
        if layer % 2 == 0:
            gb, gm = _gate_rows(p["att_b_f"][idx], 0)
            main = 3 * w_sb + 3 * w_fx
            kv_cols = (w_sb, 2 * w_sb, 3 * w_sb + w_fx, 3 * w_sb + 2 * w_fx)
            kv_heads = ((n_sb, hd_sb), (n_sb, hd_sb), (n_fx, hd_fx), (n_fx, hd_fx))
            if caches is None:
                proj, *kv_t = _proj(x, g(2), p["att_w_in"][idx], gb, gm, seq, kv_cols)
                k_sb, v_sb, k_fx, v_fx = (
                    jnp.transpose(t.reshape(bsz, nh, hd, seq), (0, 3, 1, 2)) for t, (nh, hd) in zip(kv_t, kv_heads))
            else:
                proj, = _proj(x, g(2), p["att_w_in"][idx], gb, gm)
                k_sb, v_sb, k_fx, v_fx = (
                    proj[:, c:c + nh * hd].reshape(bsz, seq, nh, hd) for c, (nh, hd) in zip(kv_cols, kv_heads))
            lf = proj[:, main:main + n_fx].reshape(bsz, seq, n_fx)
            att_rows.append((k_sb, v_sb, k_fx, v_fx, lf))
            w_out = p["att_w_out"][idx]
            if caches is None:
                cb = lambda off: off // LANES
                o_sb = _sb_prefill(proj, bsz, seq, cb(0), cb(w_sb), cb(2 * w_sb), w_sb // LANES, hd_sb)
                c = _cumsum_lanes(jnp.swapaxes(lf, 1, 2))
                o_fx = _fox_prefill(proj, c, bsz, seq, cb(3 * w_sb), cb(3 * w_sb + w_fx),
                                    cb(3 * w_sb + 2 * w_fx), w_fx // LANES, hd_fx)
                x = _outproj([o_sb, o_fx], [w_out[:w_sb], w_out[w_sb:]], g(3), x)
            else:
                assert seq == 1
                cols = (0, 3 * w_sb, 3 * w_sb + w_fx, 3 * w_sb + 2 * w_fx, main)
                o = _decode_attn(proj, page_table, idx, *caches, cols)
                x = _outproj([o], [w_out], g(3), x)
        else:
            gb, gm = _gate_rows(jnp.concatenate([p["ml_b_i"][idx], p["ml_b_f"][idx]]), n_ml)
            proj, = _proj(x, g(2), p["ml_w_in"][idx], gb, gm)
            main = 2 * n_ml * dk + 2 * n_ml * dv
            if ml_state is None:
                gates = proj[:, main:main + 2 * n_ml].reshape(bsz, seq, 2 * n_ml)
                gates = jnp.transpose(gates, (0, 2, 1))
                zeros = lambda *s: jnp.zeros(s, F32)
                gated, c_new, n_new, m_new = _mlstm_prefill(
                    proj, gates[:, :n_ml], gates[:, n_ml:], p["ml_norm_g"][idx],
                    zeros(bsz, n_ml, dv, dk), zeros(bsz, n_ml, dk), zeros(bsz, n_ml), bsz, seq, n_ml, dk, dv)
                n_new = n_new.reshape(bsz, n_ml, dk)
                m_new = m_new.reshape(bsz, n_ml)
            else:
                assert seq == 1
                gated, c_new, n_new, m_new = _mlstm_decode(
                    proj, p["ml_norm_g"][idx], ml_state[0][idx], ml_state[1][idx], ml_state[2][idx], main)
            ml_rows.append((c_new, n_new, m_new))
            x = _outproj([gated], [p["ml_w_out"][idx]], g(3), x)
        x = _ffn(x, g(4), p["ffn2_w_in"][layer], p["ffn2_w_out"][layer], g(5))
    att_new = [jnp.stack([r[i] for r in att_rows]) for i in range(5)]
    ml_new = [jnp.stack([r[i] for r in ml_rows]) for i in range(3)]
    return x.reshape(bsz, seq, d), att_new, ml_new


def kernel(x_prompt, x_sample, cache_sb_k, cache_sb_v, cache_fox_k, cache_fox_v, cache_fox_logf,
           state_mlstm_C, state_mlstm_n, state_mlstm_m, page_table, norm_g, ffn1_w_in, ffn1_w_out,
           ffn2_w_in, ffn2_w_out, att_w_in, att_b_f, att_w_out, ml_w_in, ml_b_i, ml_b_f, ml_norm_g,
           ml_w_out):
    n_sb, hd_sb = cache_sb_k.shape[-2:]
    n_fx, hd_fx = cache_fox_k.shape[-2:]
    n_ml, dv, dk = state_mlstm_C.shape[-3:]
    att_main = 3 * n_sb * hd_sb + 3 * n_fx * hd_fx
    ml_main = 2 * n_ml * dk + 2 * n_ml * dv
    p = dict(
        norm_g=norm_g, n_sb=n_sb, hd_sb=hd_sb, n_fx=n_fx, hd_fx=hd_fx, n_ml=n_ml, dk=dk, dv=dv,
        ffn1_w_in=ffn1_w_in.astype(BF16), ffn1_w_out=ffn1_w_out.astype(BF16),
        ffn2_w_in=ffn2_w_in.astype(BF16), ffn2_w_out=ffn2_w_out.astype(BF16),
        att_w_in=jnp.stack([_pad_gate_cols(w, att_main) for w in att_w_in]), att_b_f=att_b_f,
        att_w_out=att_w_out.astype(BF16),
        ml_w_in=jnp.stack([_pad_gate_cols(w, ml_main) for w in ml_w_in]), ml_b_i=ml_b_i, ml_b_f=ml_b_f,
        ml_norm_g=ml_norm_g, ml_w_out=ml_w_out.astype(BF16))
    y_p, att_p, ml_p = _trunk(x_prompt, None, None, None, p)
    caches = (cache_sb_k, cache_sb_v, cache_fox_k, cache_fox_v, cache_fox_logf)
    y_s, att_s, ml_s = _trunk(x_sample, caches, page_table,
                              (state_mlstm_C, state_mlstm_n, state_mlstm_m), p)
    return (y_p, y_s, *att_p, *ml_p, *att_s, *ml_s)
```

```python
import functools

import jax
import jax.numpy as jnp
from jax import lax
from jax.experimental import pallas as pl
from jax.experimental.pallas import tpu as pltpu

NORM_EPS = 1e-6
LANES = 128
SUBLANES = 8
VMEM_LIMIT_BYTES = 56 * 1024 * 1024
MXU_DIM = 256
SB_TQ = 1024
FOX_TQ = 512
FOX_TK = 512
MLSTM_CHUNK = 256
MASKED_MAX = -1e30
DECODE_PAGES_PER_STEP = 16

F32 = jnp.float32
BF16 = jnp.bfloat16
NT_DIMS = (((1,), (1,)), ((), ()))


def _rms(x, g):
    return x * lax.rsqrt(jnp.mean(x * x, axis=-1, keepdims=True) + NORM_EPS) * g


def _log_sigmoid(x):
    return jnp.minimum(x, 0.0) - jnp.log(1.0 + jnp.exp(-jnp.abs(x)))


def _sigmoid(x):
    return 1.0 / (1.0 + jnp.exp(-x))


def _resident(shape):
    return pl.BlockSpec(shape, lambda *_: (0,) * len(shape), pipeline_mode=pl.Buffered(1))


def _params(semantics):
    return pltpu.CompilerParams(dimension_semantics=semantics, vmem_limit_bytes=VMEM_LIMIT_BYTES)


def _ffn_kernel(x_ref, gpre_ref, win_ref, wout_ref, gpost_ref, o_ref, acc_ref, *, d_ff, tf):
    x = x_ref[...]
    h = _rms(x, gpre_ref[...]).astype(BF16)
    for c in range(d_ff // tf):
        gate = jnp.dot(h, win_ref[:, c * tf:(c + 1) * tf], preferred_element_type=F32)
        up = jnp.dot(h, win_ref[:, d_ff + c * tf:d_ff + (c + 1) * tf], preferred_element_type=F32)
        act = (gate * _sigmoid(gate) * up).astype(BF16)
        part = jnp.dot(act, wout_ref[c * tf:(c + 1) * tf, :], preferred_element_type=F32)
        if c == 0:
            acc_ref[...] = part
        else:
            acc_ref[...] += part
    o_ref[...] = x + 0.5 * _rms(acc_ref[...], gpost_ref[...])


def _ffn(x, g_pre, w_in, w_out, g_post, layer):
    m, d = x.shape
    d_ff = w_out.shape[1]
    tm = min(m, 512)
    tf = 256
    assert m % tm == 0 and d_ff % tf == 0
    of_layer = lambda *tail: pl.BlockSpec((None,) + tail, lambda i: (layer,) + (0,) * len(tail),
                                          pipeline_mode=pl.Buffered(1))
    return pl.pallas_call(
        functools.partial(_ffn_kernel, d_ff=d_ff, tf=tf),
        grid=(m // tm,),
        in_specs=[pl.BlockSpec((tm, d), lambda i: (i, 0)),
                  _resident((1, d)), of_layer(d, 2 * d_ff), of_layer(d_ff, d), _resident((1, d))],
        out_specs=pl.BlockSpec((tm, d), lambda i: (i, 0)),
        out_shape=jax.ShapeDtypeStruct((m, d), F32),
        scratch_shapes=[pltpu.VMEM((tm, d), F32)],
        compiler_params=_params(("parallel",)),
        name="ffn",
    )(x, g_pre, w_in, w_out, g_post)


def _proj_kernel(x_ref, g_ref, w_ref, gbias_ref, glogsig_ref, o_ref, *t_refs, n_main, chunk, t_cols):
    h = _rms(x_ref[...], g_ref[...]).astype(BF16)
    for c0 in range(0, n_main, chunk):
        c1 = min(c0 + chunk, n_main)
        y = jnp.dot(h, w_ref[:, c0:c1], preferred_element_type=F32)
        o_ref[:, c0:c1] = y
        if c0 in t_cols:
            t_refs[t_cols.index(c0)][...] = y.T
    y = jnp.dot(h, w_ref[:, n_main:n_main + LANES], preferred_element_type=F32) + gbias_ref[...]
    o_ref[:, n_main:n_main + LANES] = jnp.where(glogsig_ref[...] > 0.0, _log_sigmoid(y), y)


def _proj(x, g, w, gate_bias, gate_logsig, seq=None, t_cols=()):
    m, d = x.shape
    n = w.shape[1]
    n_main = n - LANES
    chunk = 512
    tm = min(m, 512)
    assert m % tm == 0 and all(c % chunk == 0 for c in t_cols)
    out_specs = [pl.BlockSpec((tm, n), lambda i: (i, 0))]
    out_shape = [jax.ShapeDtypeStruct((m, n), F32)]
    if t_cols:
        per_seq = seq // tm
        assert seq % tm == 0
        out_specs += [pl.BlockSpec((None, chunk, tm), lambda i: (i // per_seq, 0, i % per_seq))] * len(t_cols)
        out_shape += [jax.ShapeDtypeStruct((m // seq, chunk, seq), F32)] * len(t_cols)
    return pl.pallas_call(
        functools.partial(_proj_kernel, n_main=n_main, chunk=chunk, t_cols=tuple(t_cols)),
        grid=(m // tm,),
        in_specs=[pl.BlockSpec((tm, d), lambda i: (i, 0)),
                  _resident((1, d)), _resident((d, n)), _resident((1, LANES)), _resident((1, LANES))],
        out_specs=out_specs,
        out_shape=out_shape,
        compiler_params=_params(("parallel",)),
        name="proj",
    )(x, g, w, gate_bias, gate_logsig)


def _outproj_kernel(*refs, n_in):
    a_refs, w_refs = refs[:n_in], refs[n_in:2 * n_in]
    g_ref, x_ref, o_ref = refs[2 * n_in:]
    mix = None
    for a_ref, w_ref in zip(a_refs, w_refs):
        part = jnp.dot(a_ref[...].astype(BF16), w_ref[...], preferred_element_type=F32)
        mix = part if mix is None else mix + part
    o_ref[...] = x_ref[...] + _rms(mix, g_ref[...])


def _outproj(acts, ws, g, x):
    m, d = x.shape
    tm = min(m, 512)
    assert m % tm == 0
    n_in = len(acts)
    in_specs = [pl.BlockSpec((tm, a.shape[1]), lambda i: (i, 0)) for a in acts]
    in_specs += [_resident(w.shape) for w in ws]
    in_specs += [_resident((1, d)), pl.BlockSpec((tm, d), lambda i: (i, 0))]
    return pl.pallas_call(
        functools.partial(_outproj_kernel, n_in=n_in),
        grid=(m // tm,),
        in_specs=in_specs,
        out_specs=pl.BlockSpec((tm, d), lambda i: (i, 0)),
        out_shape=jax.ShapeDtypeStruct((m, d), F32),
        compiler_params=_params(("parallel",)),
        name="outproj",
    )(*acts, *ws, g, x)


def _head_lane_mask(hd, h):
    lane = lax.broadcasted_iota(jnp.int32, (1, LANES), 1)
    return (lane >= h * hd) & (lane < (h + 1) * hd)


def _sb_prefill_kernel(q_ref, k_ref, v_ref, sel_ref, o_ref, kb_s, vh_s, *, tq, tk, hd, scale):
    i = pl.program_id(2)
    heads = LANES // hd
    per_tile = tq // tk
    seq = k_ref.shape[0]

    @pl.when(i == 0)
    def _():
        def convert(c, carry):
            rows = pl.ds(pl.multiple_of(c * tk, tk), tk)
            kb_s[rows, :] = k_ref[rows, :].astype(BF16)
            v = v_ref[rows, :]
            for h in range(heads):
                vh_s[h, rows, :] = jnp.where(_head_lane_mask(hd, h), v, 0.0).astype(BF16)
            return carry
        lax.fori_loop(0, seq // tk, convert, 0)

    q = q_ref[...] * scale
    qh = [jnp.where(_head_lane_mask(hd, h), q, 0.0).astype(BF16) for h in range(heads)]

    def block(j, r0, carries, acc, diagonal):
        n = tq - r0
        keys = pl.ds(pl.multiple_of(j * tk, tk), tk)
        kb = kb_s[keys, :]
        if diagonal:
            causal = (lax.broadcasted_iota(jnp.int32, (n, tk), 1)
                      < lax.broadcasted_iota(jnp.int32, (n, tk), 0))
        new_carries = []
        part = None
        for h in range(heads):
            carry = carries[h][r0:]
            z = lax.dot_general(qh[h][r0:], kb, NT_DIMS, preferred_element_type=F32)
            log_beta = _log_sigmoid(z)
            log_1m = log_beta - z
            if diagonal:
                log_1m = jnp.where(causal, log_1m, 0.0)
            later = jnp.dot(log_1m.astype(BF16), sel_ref[...], preferred_element_type=F32)
            w = jnp.exp(log_beta + later)
            if diagonal:
                w = jnp.where(causal, w, 0.0)
            pv = jnp.exp(carry) * jnp.dot(w.astype(BF16), vh_s[h, keys, :], preferred_element_type=F32)
            part = pv if part is None else part + pv
            carry = carry + (later[:, 0:1] + log_1m[:, 0:1])
            new_carries.append(carry if r0 == 0 else jnp.concatenate([carries[h][:r0], carry], axis=0))
        acc = acc + part if r0 == 0 else jnp.concatenate([acc[:r0], acc[r0:] + part], axis=0)
        return tuple(new_carries), acc

    carries = tuple(jnp.zeros((tq, 1), F32) for _ in range(heads))
    acc = jnp.zeros((tq, LANES), F32)
    for d in reversed(range(per_tile)):
        carries, acc = block(i * per_tile + d, d * tk, carries, acc, True)

    def body(t, state):
        for u in range(per_tile):
            state = block((i - t) * per_tile - 1 - u, 0, state[0], state[1], False)
        return state

    carries, acc = lax.fori_loop(0, i, body, (carries, acc))
    o_ref[...] = acc


def _sb_prefill(proj, batch, seq, q_col, k_col, v_col, n_groups, hd):
    tq = min(seq, SB_TQ)
    tk = min(seq, MXU_DIM)
    nq = seq // tq
    assert seq % tq == 0 and tq % tk == 0
    later_sel = jnp.tril(jnp.ones((tk, tk), BF16), -1)
    return pl.pallas_call(
        functools.partial(_sb_prefill_kernel, tq=tq, tk=tk, hd=hd, scale=hd ** -0.5),
        grid=(batch, n_groups, nq),
        in_specs=[pl.BlockSpec((tq, LANES), lambda b, p, i: (b * nq + i, q_col + p)),
                  pl.BlockSpec((seq, LANES), lambda b, p, i: (b, k_col + p)),
                  pl.BlockSpec((seq, LANES), lambda b, p, i: (b, v_col + p)),
                  _resident((tk, tk))],
        out_specs=pl.BlockSpec((tq, LANES), lambda b, p, i: (b * nq + i, p)),
        out_shape=jax.ShapeDtypeStruct((batch * seq, n_groups * LANES), F32),
        scratch_shapes=[pltpu.VMEM((seq, LANES), BF16), pltpu.VMEM((LANES // hd, seq, LANES), BF16)],
        compiler_params=_params(("parallel", "parallel", "arbitrary")),
        name="sb_prefill",
    )(proj, proj, proj, later_sel)


def _cumsum_kernel(x_ref, o_ref, *, chunk):
    rows, t = x_ref.shape
    j = lax.broadcasted_iota(jnp.int32, (chunk, chunk), 0)
    s = lax.broadcasted_iota(jnp.int32, (chunk, chunk), 1)
    incl = (j <= s).astype(F32)
    carry = jnp.zeros((rows, 1), F32)
    for c in range(t // chunk):
        x = x_ref[:, c * chunk:(c + 1) * chunk]
        o_ref[:, c * chunk:(c + 1) * chunk] = carry + jnp.dot(
            x, incl, preferred_element_type=F32, precision=lax.Precision.HIGHEST)
        carry = carry + jnp.sum(x, axis=-1, keepdims=True)


def _cumsum_lanes(x):
    b, r, t = x.shape
    chunk = min(t, MXU_DIM)
    assert t % chunk == 0
    return pl.pallas_call(
        functools.partial(_cumsum_kernel, chunk=chunk),
        grid=(b,),
        in_specs=[pl.BlockSpec((None, r, t), lambda i: (i, 0, 0))],
        out_specs=pl.BlockSpec((None, r, t), lambda i: (i, 0, 0)),
        out_shape=jax.ShapeDtypeStruct((b, r, t), F32),
        compiler_params=_params(("parallel",)),
        name="cumsum",
    )(x)


def _fox_prefill_kernel(q_ref, k_ref, v_ref, cq_ref, ck_ref, o_ref, kb_s, vh_s, *, tq, tk, hd, scale):
    i = pl.program_id(2)
    heads = LANES // hd
    per_tile = tq // tk
    seq = k_ref.shape[0]
    lane = lax.broadcasted_iota(jnp.int32, (1, LANES), 1)
    ones_lane = [((h + 1) % heads) * hd for h in range(heads)]

    @pl.when(i == 0)
    def _():
        def convert(c, carry):
            rows = pl.ds(pl.multiple_of(c * tk, tk), tk)
            kb_s[rows, :] = k_ref[rows, :].astype(BF16)
            v = v_ref[rows, :]
            for h in range(heads):
                vh = jnp.where(_head_lane_mask(hd, h), v, jnp.where(lane == ones_lane[h], 1.0, 0.0))
                vh_s[h, rows, :] = vh.astype(BF16)
            return carry
        lax.fori_loop(0, seq // tk, convert, 0)

    q = q_ref[...] * scale
    qh = [jnp.where(_head_lane_mask(hd, h), q, 0.0).astype(BF16) for h in range(heads)]
    cq = [cq_ref[:, h:h + 1] for h in range(heads)]

    def block(j, r0, ms, accs, diagonal):
        n = tq - r0
        start = pl.multiple_of(j * tk, tk)
        keys = pl.ds(start, tk)
        kb = kb_s[keys, :]
        if diagonal:
            causal = (lax.broadcasted_iota(jnp.int32, (n, tk), 1)
                      <= lax.broadcasted_iota(jnp.int32, (n, tk), 0))
        new_ms, new_accs = [], []
        for h in range(heads):
            s = (lax.dot_general(qh[h][r0:], kb, NT_DIMS, preferred_element_type=F32)
                 + cq[h][r0:] - ck_ref[h:h + 1, keys])
            if diagonal:
                s = jnp.where(causal, s, -jnp.inf)
            m_old = ms[h][r0:]
            m_new = jnp.maximum(m_old, jnp.max(s, axis=-1, keepdims=True))
            p = jnp.exp(s - m_new)
            acc = (jnp.exp(m_old - m_new) * accs[h][r0:]
                   + jnp.dot(p.astype(BF16), vh_s[h, keys, :], preferred_element_type=F32))
            if r0:
                m_new = jnp.concatenate([ms[h][:r0], m_new], axis=0)
                acc = jnp.concatenate([accs[h][:r0], acc], axis=0)
            new_ms.append(m_new)
            new_accs.append(acc)
        return tuple(new_ms), tuple(new_accs)

    ms = tuple(jnp.full((tq, 1), MASKED_MAX, F32) for _ in range(heads))
    accs = tuple(jnp.zeros((tq, LANES), F32) for _ in range(heads))
    for d in reversed(range(per_tile)):
        ms, accs = block(i * per_tile + d, d * tk, ms, accs, True)

    past = i * per_tile

    def single(t, state):
        return block(past - 1 - t, 0, state[0], state[1], False)

    def pair(t, state):
        state = block(past - past % 2 - 1 - 2 * t, 0, state[0], state[1], False)
        return block(past - past % 2 - 2 - 2 * t, 0, state[0], state[1], False)

    ms, accs = lax.fori_loop(0, past % 2, single, (ms, accs))
    ms, accs = lax.fori_loop(0, past // 2, pair, (ms, accs))
    out = None
    for h in range(heads):
        denom = accs[h][:, ones_lane[h]:ones_lane[h] + 1]
        part = jnp.where(_head_lane_mask(hd, h), accs[h] / denom, 0.0)
        out = part if out is None else out + part
    o_ref[...] = out


def _fox_prefill(proj, c, batch, seq, q_col, k_col, v_col, n_groups, hd):
    tq = min(seq, FOX_TQ)
    tk = min(seq, FOX_TK)
    nq = seq // tq
    heads = LANES // hd
    assert seq % tq == 0 and tq % tk == 0 and heads >= 2
    ck = c.reshape(batch, n_groups, heads, seq)
    cq = jnp.swapaxes(ck, 2, 3)
    return pl.pallas_call(
        functools.partial(_fox_prefill_kernel, tq=tq, tk=tk, hd=hd, scale=hd ** -0.5),
        grid=(batch, n_groups, nq),
        in_specs=[pl.BlockSpec((tq, LANES), lambda b, p, i: (b * nq + i, q_col + p)),
                  pl.BlockSpec((seq, LANES), lambda b, p, i: (b, k_col + p)),
                  pl.BlockSpec((seq, LANES), lambda b, p, i: (b, v_col + p)),
                  pl.BlockSpec((None, None, tq, heads), lambda b, p, i: (b, p, i, 0)),
                  pl.BlockSpec((None, None, heads, seq), lambda b, p, i: (b, p, 0, 0))],
        out_specs=pl.BlockSpec((tq, LANES), lambda b, p, i: (b * nq + i, p)),
        out_shape=jax.ShapeDtypeStruct((batch * seq, n_groups * LANES), F32),
        scratch_shapes=[pltpu.VMEM((seq, LANES), BF16), pltpu.VMEM((heads, seq, LANES), BF16)],
        compiler_params=_params(("parallel", "parallel", "arbitrary")),
        name="fox_prefill",
    )(proj, proj, proj, cq, ck)


def _mlstm_chunk_kernel(q_ref, k_ref, v_ref, og_ref, icol_ref, fcol_ref, irow_ref, frow_ref, ng_ref,
                        c0_ref, n0_ref, m0_ref, gated_ref, c_ref, n_ref, m_ref, *, chunk, n_heads, dk, dv):
    step = pl.program_id(1)

    @pl.when(step == 0)
    def _():
        c_ref[...] = c0_ref[...]
        n_ref[...] = n0_ref[...]
        m_ref[...] = m0_ref[...]

    t_idx = lax.broadcasted_iota(jnp.int32, (chunk, chunk), 0)
    s_idx = lax.broadcasted_iota(jnp.int32, (chunk, chunk), 1)
    causal = s_idx <= t_idx
    for h in range(n_heads):
        q = q_ref[:, h * dk:(h + 1) * dk]
        k = k_ref[:, h * dk:(h + 1) * dk] * dk ** -0.5
        v = v_ref[:, h * dv:(h + 1) * dv]
        qb, kb = q.astype(BF16), k.astype(BF16)
        b_col = jnp.sum(jnp.where(causal, frow_ref[h], 0.0), axis=1, keepdims=True)
        b_row = jnp.sum(jnp.where(t_idx <= s_idx, fcol_ref[h], 0.0), axis=0, keepdims=True)
        m_prev = m_ref[h]
        n_prev = n_ref[h]
        c_prev = c_ref[h]

        d = jnp.where(causal, b_col - b_row + irow_ref[h], -jnp.inf)
        g = b_col + m_prev
        m_t = jnp.maximum(g, jnp.max(d, axis=1, keepdims=True))
        wt = jnp.exp(d - m_t) * lax.dot_general(qb, kb, NT_DIMS, preferred_element_type=F32)
        decay = jnp.exp(g - m_t)
        num = (decay * lax.dot_general(qb, c_prev.astype(BF16), NT_DIMS, preferred_element_type=F32)
               + jnp.dot(wt.astype(BF16), v.astype(BF16), preferred_element_type=F32))
        den = decay * jnp.sum(q * n_prev, axis=1, keepdims=True) + jnp.sum(wt, axis=1, keepdims=True)
        hid = num / jnp.maximum(jnp.abs(den), jnp.exp(-m_t))

        hn = _rms(hid, ng_ref[h])
        gated_ref[:, h * dv:(h + 1) * dv] = (_sigmoid(og_ref[:, h * dv:(h + 1) * dv]) * hn).astype(gated_ref.dtype)

        b_last = b_col[chunk - 1:chunk, :]
        a_col = b_last - b_col + icol_ref[h]
        m_new = jnp.maximum(b_last + m_prev, jnp.max(a_col, axis=0, keepdims=True))
        w_col = jnp.exp(a_col - m_new)
        carry_decay = jnp.exp(b_last + m_prev - m_new)
        wv_t = (w_col * v).T.astype(BF16)
        c_ref[h] = carry_decay * c_prev + jnp.dot(wv_t, kb, preferred_element_type=F32)
        n_ref[h] = carry_decay * n_prev + jnp.sum(w_col * k, axis=0, keepdims=True)
        m_ref[h] = m_new


def _mlstm_prefill(proj, gates_i, gates_f, norm_g, c0, n0, m0, batch, seq, n_heads, dk, dv):
    chunk = min(seq, MLSTM_CHUNK)
    nc = seq // chunk
    assert seq % chunk == 0 and dk == dv
    row4 = lambda a: a.reshape(batch, n_heads, 1, seq)
    col4 = lambda a: a.reshape(batch, n_heads, seq, 1)
    tok = lambda group: pl.BlockSpec((chunk, n_heads * dk), lambda b, c: (b * nc + c, group))
    colspec = pl.BlockSpec((None, n_heads, chunk, 1), lambda b, c: (b, 0, c, 0))
    rowspec = pl.BlockSpec((None, n_heads, 1, chunk), lambda b, c: (b, 0, 0, c))
    cspec = pl.BlockSpec((None, n_heads, dv, dk), lambda b, c: (b, 0, 0, 0))
    nspec = pl.BlockSpec((None, n_heads, 1, dk), lambda b, c: (b, 0, 0, 0))
    mspec = pl.BlockSpec((None, n_heads, 1, 1), lambda b, c: (b, 0, 0, 0))
    return pl.pallas_call(
        functools.partial(_mlstm_chunk_kernel, chunk=chunk, n_heads=n_heads, dk=dk, dv=dv),
        grid=(batch, nc),
        in_specs=[tok(0), tok(1), tok(2), tok(3), colspec, colspec, rowspec, rowspec,
                  _resident((n_heads, 1, dv)), cspec, nspec, mspec],
        out_specs=[pl.BlockSpec((chunk, n_heads * dv), lambda b, c: (b * nc + c, 0)), cspec, nspec, mspec],
        out_shape=[jax.ShapeDtypeStruct((batch * seq, n_heads * dv), BF16),
                   jax.ShapeDtypeStruct((batch, n_heads, dv, dk), F32),
                   jax.ShapeDtypeStruct((batch, n_heads, 1, dk), F32),
                   jax.ShapeDtypeStruct((batch, n_heads, 1, 1), F32)],
        compiler_params=_params(("parallel", "arbitrary")),
        name="mlstm_chunk",
    )(proj, proj, proj, proj, col4(gates_i), col4(gates_f), row4(gates_i), row4(gates_f),
      norm_g.reshape(n_heads, 1, dv), c0, n0.reshape(batch, n_heads, 1, dk), m0.reshape(batch, n_heads, 1, 1))


def _block_diag_mask(n_heads, hd):
    r = lax.broadcasted_iota(jnp.int32, (n_heads, n_heads * hd), 0)
    lane = lax.broadcasted_iota(jnp.int32, (n_heads, n_heads * hd), 1)
    return (lane >= r * hd) & (lane < (r + 1) * hd)


def _lanes_to_col(row, n):
    r = lax.broadcasted_iota(jnp.int32, (n, row.shape[1]), 0)
    lane = lax.broadcasted_iota(jnp.int32, (n, row.shape[1]), 1)
    return jnp.sum(jnp.where(r == lane, row, 0.0), axis=1, keepdims=True)


def _decode_attn_kernel(pt_ref, row_ref, *refs, group, n_sb, hd_sb, n_fx, hd_fx, cols):
    del pt_ref
    ksb_refs, vsb_refs, kfx_refs, vfx_refs, lft_refs = (refs[g * group:(g + 1) * group] for g in range(5))
    o_ref, qsb_s, qfx_s, carry_sb_s, acc_sb_s, carry_fx_s, m_s, l_s, acc_fx_s = refs[5 * group:]
    p = pl.program_id(1)
    page = ksb_refs[0].shape[1]
    w_sb, w_fx = n_sb * hd_sb, n_fx * hd_fx
    q_sb_col, q_fx_col, k_fx_col, v_fx_col, gate_col = cols
    mask_sb = _block_diag_mask(n_sb, hd_sb)
    mask_fx = _block_diag_mask(n_fx, hd_fx)

    @pl.when(p == 0)
    def _():
        row = row_ref[...]
        q_sb = jnp.where(mask_sb, row[:, q_sb_col:q_sb_col + w_sb] * hd_sb ** -0.5, 0.0)
        q_fx = jnp.where(mask_fx, row[:, q_fx_col:q_fx_col + w_fx] * hd_fx ** -0.5, 0.0)
        qsb_s[...] = q_sb.astype(BF16)
        qfx_s[...] = q_fx.astype(BF16)
        carry_sb_s[...] = jnp.zeros_like(carry_sb_s)
        acc_sb_s[...] = jnp.zeros_like(acc_sb_s)
        k_cur = row[:, k_fx_col:k_fx_col + w_fx]
        v_cur = row[:, v_fx_col:v_fx_col + w_fx]
        m_s[...] = jnp.sum(q_fx * k_cur, axis=1, keepdims=True)
        l_s[...] = jnp.ones_like(l_s)
        acc_fx_s[...] = jnp.broadcast_to(v_cur, acc_fx_s.shape)
        carry_fx_s[...] = _lanes_to_col(row[:, gate_col:gate_col + LANES], n_fx)

    j = lax.broadcasted_iota(jnp.int32, (page, page), 0)
    s = lax.broadcasted_iota(jnp.int32, (page, page), 1)
    later_sel = (j > s).astype(F32)

    def lanes(refs, dtype):
        return jnp.concatenate([r[...].astype(dtype) for r in refs], axis=1)

    def later_bias(x, carry):
        parts = [x[:, g * page:(g + 1) * page] for g in range(group)]
        inside = jnp.dot(jnp.concatenate(parts, axis=0), later_sel, preferred_element_type=F32,
                         precision=lax.Precision.HIGHEST)
        rows = x.shape[0]
        out = []
        for g in range(group):
            out.append(inside[g * rows:(g + 1) * rows] + carry)
            carry = carry + jnp.sum(parts[g], axis=1, keepdims=True)
        return jnp.concatenate(out, axis=1), carry

    z = jnp.dot(qsb_s[...], lanes(ksb_refs, BF16), preferred_element_type=F32)
    log_beta = _log_sigmoid(z)
    later, carry_sb_s[...] = later_bias(log_beta - z, carry_sb_s[...])
    w = jnp.exp(log_beta + later)
    acc_sb_s[...] += lax.dot_general(w.astype(BF16), lanes(vsb_refs, BF16), NT_DIMS, preferred_element_type=F32)

    bias, carry_fx_s[...] = later_bias(lanes(lft_refs, F32), carry_fx_s[...])
    sc = jnp.dot(qfx_s[...], lanes(kfx_refs, BF16), preferred_element_type=F32) + bias
    m_new = jnp.maximum(m_s[...], jnp.max(sc, axis=1, keepdims=True))
    alpha = jnp.exp(m_s[...] - m_new)
    pr = jnp.exp(sc - m_new)
    l_s[...] = alpha * l_s[...] + jnp.sum(pr, axis=1, keepdims=True)
    acc_fx_s[...] = alpha * acc_fx_s[...] + lax.dot_general(
        pr.astype(BF16), lanes(vfx_refs, BF16), NT_DIMS, preferred_element_type=F32)
    m_s[...] = m_new

    @pl.when(p == pl.num_programs(1) - 1)
    def _():
        o_sb = jnp.sum(jnp.where(mask_sb, acc_sb_s[...], 0.0), axis=0, keepdims=True)
        o_fx = jnp.sum(jnp.where(mask_fx, acc_fx_s[...] / l_s[...], 0.0), axis=0, keepdims=True)
        o_ref[:, 0:w_sb] = o_sb
        o_ref[:, w_sb:w_sb + w_fx] = o_fx


def _decode_attn(proj, page_table, layer, cache_sb_k, cache_sb_v, cache_fox_k, cache_fox_v, cache_fox_logf, cols):
    b, n = proj.shape
    _, n_phys, page, n_sb, hd_sb = cache_sb_k.shape
    n_fx, hd_fx = cache_fox_k.shape[-2:]
    n_pages = page_table.shape[1]
    w_sb, w_fx = n_sb * hd_sb, n_fx * hd_fx
    flat = lambda a: jnp.transpose(a, (0, 1, 3, 4, 2)).reshape(a.shape[0], n_phys, -1, page)
    lft = jnp.swapaxes(cache_fox_logf, 2, 3)

    group = min(n_pages, DECODE_PAGES_PER_STEP)
    assert n_pages % group == 0

    def paged(rows, g):
        return pl.BlockSpec((None, None, rows, page),
                            lambda i, p, pt: (layer, pt[i, n_pages - 1 - p * group - g], 0, 0))

    caches = (flat(cache_sb_k), flat(cache_sb_v), flat(cache_fox_k), flat(cache_fox_v), lft)
    rows = (w_sb, w_sb, w_fx, w_fx, n_fx)
    grid_spec = pltpu.PrefetchScalarGridSpec(
        num_scalar_prefetch=1,
        grid=(b, n_pages // group),
        in_specs=[pl.BlockSpec((None, 1, n), lambda i, p, pt: (i, 0, 0))]
        + [paged(r, g) for r in rows for g in range(group)],
        out_specs=pl.BlockSpec((None, 1, w_sb + w_fx), lambda i, p, pt: (i, 0, 0)),
        scratch_shapes=[pltpu.VMEM((n_sb, w_sb), BF16), pltpu.VMEM((n_fx, w_fx), BF16),
                        pltpu.VMEM((n_sb, 1), F32), pltpu.VMEM((n_sb, w_sb), F32),
                        pltpu.VMEM((n_fx, 1), F32), pltpu.VMEM((n_fx, 1), F32), pltpu.VMEM((n_fx, 1), F32),
                        pltpu.VMEM((n_fx, w_fx), F32)])
    out = pl.pallas_call(
        functools.partial(_decode_attn_kernel, group=group, n_sb=n_sb, hd_sb=hd_sb, n_fx=n_fx, hd_fx=hd_fx,
                          cols=cols),
        grid_spec=grid_spec,
        out_shape=jax.ShapeDtypeStruct((b, 1, w_sb + w_fx), F32),
        compiler_params=_params(("parallel", "arbitrary")),
        name="decode_attn",
    )(page_table, proj.reshape(b, 1, n), *[c for c in caches for _ in range(group)])
    return out.reshape(b, w_sb + w_fx)


def _mlstm_step_kernel(row_ref, ng_ref, c0_ref, n0_ref, m0_ref, gated_ref, c_ref, n_ref, m_ref,
                       *, n_heads, dk, dv, gate_col):
    row = row_ref[...]
    gates = row[:, gate_col:gate_col + LANES]
    eye = (lax.broadcasted_iota(jnp.int32, (dv, dv), 0) == lax.broadcasted_iota(jnp.int32, (dv, dv), 1))
    for h in range(n_heads):
        q = row[:, h * dk:(h + 1) * dk]
        k = row[:, (n_heads + h) * dk:(n_heads + h + 1) * dk] * dk ** -0.5
        v = row[:, 2 * n_heads * dk + h * dv:2 * n_heads * dk + (h + 1) * dv]
        og = row[:, 2 * n_heads * dk + (n_heads + h) * dv:2 * n_heads * dk + (n_heads + h + 1) * dv]
        i_log = gates[:, h:h + 1]
        f_log = gates[:, n_heads + h:n_heads + h + 1]
        c_prev = c0_ref[h]
        n_prev = n0_ref[h:h + 1, :]
        m_prev = m0_ref[h:h + 1, :]
        g = f_log + m_prev
        m_t = jnp.maximum(g, i_log)
        w_in = jnp.exp(i_log - m_t)
        decay = jnp.exp(g - m_t)
        qk = jnp.sum(q * k, axis=1, keepdims=True)
        q8 = jnp.broadcast_to(q, (SUBLANES, dk)).astype(BF16)
        cq = lax.dot_general(q8, c_prev.astype(BF16), NT_DIMS, preferred_element_type=F32)[0:1, :]
        num = decay * cq + (w_in * qk) * v
        den = decay * jnp.sum(q * n_prev, axis=1, keepdims=True) + w_in * qk
        hid = num / jnp.maximum(jnp.abs(den), jnp.exp(-m_t))
        hn = _rms(hid, ng_ref[h:h + 1, :])
        gated_ref[:, h * dv:(h + 1) * dv] = _sigmoid(og) * hn
        v_col = jnp.sum(jnp.where(eye, v, 0.0), axis=1, keepdims=True)
        c_ref[h] = decay * c_prev + v_col * (w_in * k)
        n_ref[h:h + 1, :] = decay * n_prev + w_in * k
        m_ref[h:h + 1, :] = m_t


def _mlstm_decode(proj, norm_g, c0, n0, m0, gate_col):
    b, n = proj.shape
    _, n_heads, dv, dk = c0.shape
    seq_spec = lambda *tail: pl.BlockSpec((None,) + tail, lambda i: (i,) + (0,) * len(tail))
    gated, c, nn, m = pl.pallas_call(
        functools.partial(_mlstm_step_kernel, n_heads=n_heads, dk=dk, dv=dv, gate_col=gate_col),
        grid=(b,),
        in_specs=[seq_spec(1, n), _resident((n_heads, dv)), seq_spec(n_heads, dv, dk),
                  seq_spec(n_heads, dk), seq_spec(n_heads, 1)],
        out_specs=[seq_spec(1, n_heads * dv), seq_spec(n_heads, dv, dk), seq_spec(n_heads, dk),
                   seq_spec(n_heads, 1)],
        out_shape=[jax.ShapeDtypeStruct((b, 1, n_heads * dv), F32),
                   jax.ShapeDtypeStruct((b, n_heads, dv, dk), F32),
                   jax.ShapeDtypeStruct((b, n_heads, dk), F32),
                   jax.ShapeDtypeStruct((b, n_heads, 1), F32)],
        compiler_params=_params(("parallel",)),
        name="mlstm_step",
    )(proj.reshape(b, 1, n), norm_g, c0, n0, m0.reshape(b, n_heads, 1))
    return gated.reshape(b, n_heads * dv), c, nn, m.reshape(b, n_heads)


def _pad_gate_cols(w, n_main):
    n_gates = w.shape[1] - n_main
    return jnp.pad(w, ((0, 0), (0, LANES - n_gates))).astype(BF16)


def _gate_rows(bias, logsig_from):
    n = bias.shape[0]
    gb = jnp.pad(bias.astype(F32), (0, LANES - n)).reshape(1, LANES)
    gm = (jnp.arange(LANES) >= logsig_from) & (jnp.arange(LANES) < n)
    return gb, gm.astype(F32).reshape(1, LANES)


def _trunk(x, caches, page_table, ml_state, p):
    bsz, seq, d = x.shape
    m = bsz * seq
    x = x.reshape(m, d)
    depth = p["norm_g"].shape[0]
    n_sb, hd_sb, n_fx, hd_fx = p["n_sb"], p["hd_sb"], p["n_fx"], p["hd_fx"]
    w_sb, w_fx = n_sb * hd_sb, n_fx * hd_fx
    n_ml, dk, dv = p["n_ml"], p["dk"], p["dv"]
    att_rows, ml_rows = [], []
    for layer in range(depth):
        idx = layer // 2
        g = lambda j: p["norm_g"][layer, j].reshape(1, d)
        x = _ffn(x, g(0), p["ffn1_w_in"], p["ffn1_w_out"], g(1), layer)
        if layer % 2 == 0:
            gb, gm = _gate_rows(p["att_b_f"][idx], 0)
            main = 3 * w_sb + 3 * w_fx
            kv_cols = (w_sb, 2 * w_sb, 3 * w_sb + w_fx, 3 * w_sb + 2 * w_fx)
            kv_heads = ((n_sb, hd_sb), (n_sb, hd_sb), (n_fx, hd_fx), (n_fx, hd_fx))
            if caches is None:
                proj, *kv_t = _proj(x, g(2), p["att_w_in"][idx], gb, gm, seq, kv_cols)
                k_sb, v_sb, k_fx, v_fx = (
                    jnp.transpose(t.reshape(bsz, nh, hd, seq), (0, 3, 1, 2)) for t, (nh, hd) in zip(kv_t, kv_heads))
            else:
                proj, = _proj(x, g(2), p["att_w_in"][idx], gb, gm)
                k_sb, v_sb, k_fx, v_fx = (
                    proj[:, c:c + nh * hd].reshape(bsz, seq, nh, hd) for c, (nh, hd) in zip(kv_cols, kv_heads))
            lf = proj[:, main:main + n_fx].reshape(bsz, seq, n_fx)
            att_rows.append((k_sb, v_sb, k_fx, v_fx, lf))
            w_out = p["att_w_out"][idx]
            if caches is None:
                cb = lambda off: off // LANES
                o_sb = _sb_prefill(proj, bsz, seq, cb(0), cb(w_sb), cb(2 * w_sb), w_sb // LANES, hd_sb)
                c = _cumsum_lanes(jnp.swapaxes(lf, 1, 2))
                o_fx = _fox_prefill(proj, c, bsz, seq, cb(3 * w_sb), cb(3 * w_sb + w_fx),
                                    cb(3 * w_sb + 2 * w_fx), w_fx // LANES, hd_fx)
                x = _outproj([o_sb, o_fx], [w_out[:w_sb], w_out[w_sb:]], g(3), x)
            else:
                assert seq == 1
                cols = (0, 3 * w_sb, 3 * w_sb + w_fx, 3 * w_sb + 2 * w_fx, main)
                o = _decode_attn(proj, page_table, idx, *caches, cols)
                x = _outproj([o], [w_out], g(3), x)
        else:
            gb, gm = _gate_rows(jnp.concatenate([p["ml_b_i"][idx], p["ml_b_f"][idx]]), n_ml)
            proj, = _proj(x, g(2), p["ml_w_in"][idx], gb, gm)
            main = 2 * n_ml * dk + 2 * n_ml * dv
            if ml_state is None:
                gates = proj[:, main:main + 2 * n_ml].reshape(bsz, seq, 2 * n_ml)
                gates = jnp.transpose(gates, (0, 2, 1))
                zeros = lambda *s: jnp.zeros(s, F32)
                gated, c_new, n_new, m_new = _mlstm_prefill(
                    proj, gates[:, :n_ml], gates[:, n_ml:], p["ml_norm_g"][idx],
                    zeros(bsz, n_ml, dv, dk), zeros(bsz, n_ml, dk), zeros(bsz, n_ml), bsz, seq, n_ml, dk, dv)
                n_new = n_new.reshape(bsz, n_ml, dk)
                m_new = m_new.reshape(bsz, n_ml)
            else:
                assert seq == 1
                gated, c_new, n_new, m_new = _mlstm_decode(
                    proj, p["ml_norm_g"][idx], ml_state[0][idx], ml_state[1][idx], ml_state[2][idx], main)
            ml_rows.append((c_new, n_new, m_new))
            x = _outproj([gated], [p["ml_w_out"][idx]], g(3), x)
        x = _ffn(x, g(4), p["ffn2_w_in"], p["ffn2_w_out"], g(5), layer)
    att_new = [jnp.stack([r[i] for r in att_rows]) for i in range(5)]
    ml_new = [jnp.stack([r[i] for r in ml_rows]) for i in range(3)]
    return x.reshape(bsz, seq, d), att_new, ml_new


def kernel(x_prompt, x_sample, cache_sb_k, cache_sb_v, cache_fox_k, cache_fox_v, cache_fox_logf,
           state_mlstm_C, state_mlstm_n, state_mlstm_m, page_table, norm_g, ffn1_w_in, ffn1_w_out,
           ffn2_w_in, ffn2_w_out, att_w_in, att_b_f, att_w_out, ml_w_in, ml_b_i, ml_b_f, ml_norm_g,
           ml_w_out):
    n_sb, hd_sb = cache_sb_k.shape[-2:]
    n_fx, hd_fx = cache_fox_k.shape[-2:]
    n_ml, dv, dk = state_mlstm_C.shape[-3:]
    att_main = 3 * n_sb * hd_sb + 3 * n_fx * hd_fx
    ml_main = 2 * n_ml * dk + 2 * n_ml * dv
    p = dict(
        norm_g=norm_g, n_sb=n_sb, hd_sb=hd_sb, n_fx=n_fx, hd_fx=hd_fx, n_ml=n_ml, dk=dk, dv=dv,
        ffn1_w_in=ffn1_w_in.astype(BF16), ffn1_w_out=ffn1_w_out.astype(BF16),
        ffn2_w_in=ffn2_w_in.astype(BF16), ffn2_w_out=ffn2_w_out.astype(BF16),
        att_w_in=jnp.stack([_pad_gate_cols(w, att_main) for w in att_w_in]), att_b_f=att_b_f,
        att_w_out=att_w_out.astype(BF16),
        ml_w_in=jnp.stack([_pad_gate_cols(w, ml_main) for w in ml_w_in]), ml_b_i=ml_b_i, ml_b_f=ml_b_f,
        ml_norm_g=ml_norm_g, ml_w_out=ml_w_out.astype(BF16))
    y_p, att_p, ml_p = _trunk(x_prompt, None, None, None, p)
    caches = (cache_sb_k, cache_sb_v, cache_fox_k, cache_fox_v, cache_fox_logf)
    y_s, att_s, ml_s = _trunk(x_sample, caches, page_table,
                              (state_mlstm_C, state_mlstm_n, state_mlstm_m), p)
    return (y_p, y_s, *att_p, *ml_p, *att_s, *ml_s)
```

```python
import functools

import jax
import jax.numpy as jnp
from jax import lax
from jax.experimental import pallas as pl
from jax.experimental.pallas import tpu as pltpu

NORM_EPS = 1e-6
LANES = 128
SUBLANES = 8
VMEM_LIMIT_BYTES = 56 * 1024 * 1024
MXU_DIM = 256
SB_TQ = 1024
FOX_TQ = 512
FOX_TK = 512
FOX_UNROLL = 4
MLSTM_CHUNK = 256
MASKED_MAX = -1e30
DECODE_PAGES_PER_STEP = 16

F32 = jnp.float32
BF16 = jnp.bfloat16
NT_DIMS = (((1,), (1,)), ((), ()))


def _rms(x, g):
    return x * lax.rsqrt(jnp.mean(x * x, axis=-1, keepdims=True) + NORM_EPS) * g


def _log_sigmoid(x):
    return jnp.minimum(x, 0.0) - jnp.log(1.0 + jnp.exp(-jnp.abs(x)))


def _sigmoid(x):
    return 1.0 / (1.0 + jnp.exp(-x))


def _resident(shape):
    return pl.BlockSpec(shape, lambda *_: (0,) * len(shape), pipeline_mode=pl.Buffered(1))


def _params(semantics):
    return pltpu.CompilerParams(dimension_semantics=semantics, vmem_limit_bytes=VMEM_LIMIT_BYTES)


def _ffn_kernel(*refs, d_ff, tf, n_mix):
    x_ref, refs = refs[0], refs[1:]
    x = x_ref[...]
    if n_mix:
        a_refs, w_refs, gmix_ref = refs[:n_mix], refs[n_mix:2 * n_mix], refs[2 * n_mix]
        refs = refs[2 * n_mix + 1:]
        mix = None
        for a_ref, w_ref in zip(a_refs, w_refs):
            part = jnp.dot(a_ref[...].astype(BF16), w_ref[...], preferred_element_type=F32)
            mix = part if mix is None else mix + part
        x = x + _rms(mix, gmix_ref[...])
    gpre_ref, win_ref, wout_ref, gpost_ref, o_ref, acc_ref = refs
    h = _rms(x, gpre_ref[...]).astype(BF16)
    for c in range(d_ff // tf):
        gate = jnp.dot(h, win_ref[:, c * tf:(c + 1) * tf], preferred_element_type=F32)
        up = jnp.dot(h, win_ref[:, d_ff + c * tf:d_ff + (c + 1) * tf], preferred_element_type=F32)
        act = (gate * _sigmoid(gate) * up).astype(BF16)
        part = jnp.dot(act, wout_ref[c * tf:(c + 1) * tf, :], preferred_element_type=F32)
        if c == 0:
            acc_ref[...] = part
        else:
            acc_ref[...] += part
    o_ref[...] = x + 0.5 * _rms(acc_ref[...], gpost_ref[...])


def _ffn(x, g_pre, w_in, w_out, g_post, layer, mixer=None):
    m, d = x.shape
    d_ff = w_out.shape[1]
    tm = min(m, 512)
    tf = 256
    assert m % tm == 0 and d_ff % tf == 0
    of_layer = lambda *tail: pl.BlockSpec((None,) + tail, lambda i: (layer,) + (0,) * len(tail),
                                          pipeline_mode=pl.Buffered(1))
    acts, ws, g_mix = mixer if mixer else ((), (), None)
    mix_specs = [pl.BlockSpec((tm, a.shape[1]), lambda i: (i, 0)) for a in acts]
    mix_specs += [_resident(w.shape) for w in ws] + ([_resident((1, d))] if mixer else [])
    mix_args = [*acts, *ws] + ([g_mix] if mixer else [])
    return pl.pallas_call(
        functools.partial(_ffn_kernel, d_ff=d_ff, tf=tf, n_mix=len(acts)),
        grid=(m // tm,),
        in_specs=[pl.BlockSpec((tm, d), lambda i: (i, 0))] + mix_specs
        + [_resident((1, d)), of_layer(d, 2 * d_ff), of_layer(d_ff, d), _resident((1, d))],
        out_specs=pl.BlockSpec((tm, d), lambda i: (i, 0)),
        out_shape=jax.ShapeDtypeStruct((m, d), F32),
        scratch_shapes=[pltpu.VMEM((tm, d), F32)],
        compiler_params=_params(("parallel",)),
        name="ffn",
    )(x, *mix_args, g_pre, w_in, w_out, g_post)


def _proj_kernel(x_ref, g_ref, w_ref, gbias_ref, glogsig_ref, o_ref, *t_refs, n_main, chunk, t_cols):
    h = _rms(x_ref[...], g_ref[...]).astype(BF16)
    for c0 in range(0, n_main, chunk):
        c1 = min(c0 + chunk, n_main)
        y = jnp.dot(h, w_ref[:, c0:c1], preferred_element_type=F32)
        o_ref[:, c0:c1] = y
        if c0 in t_cols:
            t_refs[t_cols.index(c0)][...] = y.T
    y = jnp.dot(h, w_ref[:, n_main:n_main + LANES], preferred_element_type=F32) + gbias_ref[...]
    o_ref[:, n_main:n_main + LANES] = jnp.where(glogsig_ref[...] > 0.0, _log_sigmoid(y), y)


def _proj(x, g, w, gate_bias, gate_logsig, seq=None, t_cols=()):
    m, d = x.shape
    n = w.shape[1]
    n_main = n - LANES
    chunk = 512
    tm = min(m, 512)
    assert m % tm == 0 and all(c % chunk == 0 for c in t_cols)
    out_specs = [pl.BlockSpec((tm, n), lambda i: (i, 0))]
    out_shape = [jax.ShapeDtypeStruct((m, n), F32)]
    if t_cols:
        per_seq = seq // tm
        assert seq % tm == 0
        out_specs += [pl.BlockSpec((None, chunk, tm), lambda i: (i // per_seq, 0, i % per_seq))] * len(t_cols)
        out_shape += [jax.ShapeDtypeStruct((m // seq, chunk, seq), F32)] * len(t_cols)
    return pl.pallas_call(
        functools.partial(_proj_kernel, n_main=n_main, chunk=chunk, t_cols=tuple(t_cols)),
        grid=(m // tm,),
        in_specs=[pl.BlockSpec((tm, d), lambda i: (i, 0)),
                  _resident((1, d)), _resident((d, n)), _resident((1, LANES)), _resident((1, LANES))],
        out_specs=out_specs,
        out_shape=out_shape,
        compiler_params=_params(("parallel",)),
        name="proj",
    )(x, g, w, gate_bias, gate_logsig)


def _head_lane_mask(hd, h):
    lane = lax.broadcasted_iota(jnp.int32, (1, LANES), 1)
    return (lane >= h * hd) & (lane < (h + 1) * hd)


def _sb_prefill_kernel(q_ref, k_ref, v_ref, sel_ref, o_ref, kb_s, vh_s, *, tq, tk, hd, scale):
    i = pl.program_id(2)
    heads = LANES // hd
    per_tile = tq // tk
    seq = k_ref.shape[0]

    @pl.when(i == 0)
    def _():
        def convert(c, carry):
            rows = pl.ds(pl.multiple_of(c * tk, tk), tk)
            kb_s[rows, :] = k_ref[rows, :].astype(BF16)
            v = v_ref[rows, :]
            for h in range(heads):
                vh_s[h, rows, :] = jnp.where(_head_lane_mask(hd, h), v, 0.0).astype(BF16)
            return carry
        lax.fori_loop(0, seq // tk, convert, 0)

    q = q_ref[...] * scale
    qh = [jnp.where(_head_lane_mask(hd, h), q, 0.0).astype(BF16) for h in range(heads)]

    def block(j, r0, carries, acc, diagonal):
        n = tq - r0
        keys = pl.ds(pl.multiple_of(j * tk, tk), tk)
        kb = kb_s[keys, :]
        if diagonal:
            causal = (lax.broadcasted_iota(jnp.int32, (n, tk), 1)
                      < lax.broadcasted_iota(jnp.int32, (n, tk), 0))
        new_carries = []
        part = None
        for h in range(heads):
            carry = carries[h][r0:]
            z = lax.dot_general(qh[h][r0:], kb, NT_DIMS, preferred_element_type=F32)
            log_beta = _log_sigmoid(z)
            log_1m = log_beta - z
            if diagonal:
                log_1m = jnp.where(causal, log_1m, 0.0)
            later = jnp.dot(log_1m.astype(BF16), sel_ref[...], preferred_element_type=F32)
            w = jnp.exp(log_beta + later)
            if diagonal:
                w = jnp.where(causal, w, 0.0)
            pv = jnp.exp(carry) * jnp.dot(w.astype(BF16), vh_s[h, keys, :], preferred_element_type=F32)
            part = pv if part is None else part + pv
            carry = carry + (later[:, 0:1] + log_1m[:, 0:1])
            new_carries.append(carry if r0 == 0 else jnp.concatenate([carries[h][:r0], carry], axis=0))
        acc = acc + part if r0 == 0 else jnp.concatenate([acc[:r0], acc[r0:] + part], axis=0)
        return tuple(new_carries), acc

    carries = tuple(jnp.zeros((tq, 1), F32) for _ in range(heads))
    acc = jnp.zeros((tq, LANES), F32)
    for d in reversed(range(per_tile)):
        carries, acc = block(i * per_tile + d, d * tk, carries, acc, True)

    def body(t, state):
        for u in range(per_tile):
            state = block((i - t) * per_tile - 1 - u, 0, state[0], state[1], False)
        return state

    carries, acc = lax.fori_loop(0, i, body, (carries, acc))
    o_ref[...] = acc


def _sb_prefill(proj, batch, seq, q_col, k_col, v_col, n_groups, hd):
    tq = min(seq, SB_TQ)
    tk = min(seq, MXU_DIM)
    nq = seq // tq
    assert seq % tq == 0 and tq % tk == 0
    later_sel = jnp.tril(jnp.ones((tk, tk), BF16), -1)
    return pl.pallas_call(
        functools.partial(_sb_prefill_kernel, tq=tq, tk=tk, hd=hd, scale=hd ** -0.5),
        grid=(batch, n_groups, nq),
        in_specs=[pl.BlockSpec((tq, LANES), lambda b, p, i: (b * nq + i, q_col + p)),
                  pl.BlockSpec((seq, LANES), lambda b, p, i: (b, k_col + p)),
                  pl.BlockSpec((seq, LANES), lambda b, p, i: (b, v_col + p)),
                  _resident((tk, tk))],
        out_specs=pl.BlockSpec((tq, LANES), lambda b, p, i: (b * nq + i, p)),
        out_shape=jax.ShapeDtypeStruct((batch * seq, n_groups * LANES), F32),
        scratch_shapes=[pltpu.VMEM((seq, LANES), BF16), pltpu.VMEM((LANES // hd, seq, LANES), BF16)],
        compiler_params=_params(("parallel", "parallel", "arbitrary")),
        name="sb_prefill",
    )(proj, proj, proj, later_sel)


def _cumsum_kernel(x_ref, o_ref, *, chunk):
    rows, t = x_ref.shape
    j = lax.broadcasted_iota(jnp.int32, (chunk, chunk), 0)
    s = lax.broadcasted_iota(jnp.int32, (chunk, chunk), 1)
    incl = (j <= s).astype(F32)
    carry = jnp.zeros((rows, 1), F32)
    for c in range(t // chunk):
        x = x_ref[:, c * chunk:(c + 1) * chunk]
        o_ref[:, c * chunk:(c + 1) * chunk] = carry + jnp.dot(
            x, incl, preferred_element_type=F32, precision=lax.Precision.HIGHEST)
        carry = carry + jnp.sum(x, axis=-1, keepdims=True)


def _cumsum_lanes(x):
    b, r, t = x.shape
    chunk = min(t, MXU_DIM)
    assert t % chunk == 0
    return pl.pallas_call(
        functools.partial(_cumsum_kernel, chunk=chunk),
        grid=(b,),
        in_specs=[pl.BlockSpec((None, r, t), lambda i: (i, 0, 0))],
        out_specs=pl.BlockSpec((None, r, t), lambda i: (i, 0, 0)),
        out_shape=jax.ShapeDtypeStruct((b, r, t), F32),
        compiler_params=_params(("parallel",)),
        name="cumsum",
    )(x)


def _fox_prefill_kernel(q_ref, k_ref, v_ref, cq_ref, ck_ref, o_ref, kb_s, vh_s, *, tq, tk, hd, scale):
    i = pl.program_id(2)
    heads = LANES // hd
    per_tile = tq // tk
    seq = k_ref.shape[0]
    lane = lax.broadcasted_iota(jnp.int32, (1, LANES), 1)
    ones_lane = [((h + 1) % heads) * hd for h in range(heads)]

    @pl.when(i == 0)
    def _():
        def convert(c, carry):
            rows = pl.ds(pl.multiple_of(c * tk, tk), tk)
            kb_s[rows, :] = k_ref[rows, :].astype(BF16)
            v = v_ref[rows, :]
            for h in range(heads):
                vh = jnp.where(_head_lane_mask(hd, h), v, jnp.where(lane == ones_lane[h], 1.0, 0.0))
                vh_s[h, rows, :] = vh.astype(BF16)
            return carry
        lax.fori_loop(0, seq // tk, convert, 0)

    q = q_ref[...] * scale
    qh = [jnp.where(_head_lane_mask(hd, h), q, 0.0).astype(BF16) for h in range(heads)]
    cq = [cq_ref[:, h:h + 1] for h in range(heads)]

    def block(j, r0, ms, accs, diagonal):
        n = tq - r0
        start = pl.multiple_of(j * tk, tk)
        keys = pl.ds(start, tk)
        kb = kb_s[keys, :]
        if diagonal:
            causal = (lax.broadcasted_iota(jnp.int32, (n, tk), 1)
                      <= lax.broadcasted_iota(jnp.int32, (n, tk), 0))
        new_ms, new_accs = [], []
        for h in range(heads):
            s = (lax.dot_general(qh[h][r0:], kb, NT_DIMS, preferred_element_type=F32)
                 + cq[h][r0:] - ck_ref[h:h + 1, keys])
            if diagonal:
                s = jnp.where(causal, s, -jnp.inf)
            m_old = ms[h][r0:]
            m_new = jnp.maximum(m_old, jnp.max(s, axis=-1, keepdims=True))
            p = jnp.exp(s - m_new)
            acc = (jnp.exp(m_old - m_new) * accs[h][r0:]
                   + jnp.dot(p.astype(BF16), vh_s[h, keys, :], preferred_element_type=F32))
            if r0:
                m_new = jnp.concatenate([ms[h][:r0], m_new], axis=0)
                acc = jnp.concatenate([accs[h][:r0], acc], axis=0)
            new_ms.append(m_new)
            new_accs.append(acc)
        return tuple(new_ms), tuple(new_accs)

    ms = tuple(jnp.full((tq, 1), MASKED_MAX, F32) for _ in range(heads))
    accs = tuple(jnp.zeros((tq, LANES), F32) for _ in range(heads))
    for d in reversed(range(per_tile)):
        ms, accs = block(i * per_tile + d, d * tk, ms, accs, True)

    past = i * per_tile

    def single(t, state):
        return block(past - 1 - t, 0, state[0], state[1], False)

    def group(t, state):
        for u in range(FOX_UNROLL):
            state = block(past - past % FOX_UNROLL - 1 - u - FOX_UNROLL * t, 0, state[0], state[1], False)
        return state

    ms, accs = lax.fori_loop(0, past % FOX_UNROLL, single, (ms, accs))
    ms, accs = lax.fori_loop(0, past // FOX_UNROLL, group, (ms, accs))
    out = None
    for h in range(heads):
        denom = accs[h][:, ones_lane[h]:ones_lane[h] + 1]
        part = jnp.where(_head_lane_mask(hd, h), accs[h] / denom, 0.0)
        out = part if out is None else out + part
    o_ref[...] = out


def _fox_prefill(proj, c, batch, seq, q_col, k_col, v_col, n_groups, hd):
    tq = min(seq, FOX_TQ)
    tk = min(seq, FOX_TK)
    nq = seq // tq
    heads = LANES // hd
    assert seq % tq == 0 and tq % tk == 0 and heads >= 2
    ck = c.reshape(batch, n_groups, heads, seq)
    cq = jnp.swapaxes(ck, 2, 3)
    return pl.pallas_call(
        functools.partial(_fox_prefill_kernel, tq=tq, tk=tk, hd=hd, scale=hd ** -0.5),
        grid=(batch, n_groups, nq),
        in_specs=[pl.BlockSpec((tq, LANES), lambda b, p, i: (b * nq + i, q_col + p)),
                  pl.BlockSpec((seq, LANES), lambda b, p, i: (b, k_col + p)),
                  pl.BlockSpec((seq, LANES), lambda b, p, i: (b, v_col + p)),
                  pl.BlockSpec((None, None, tq, heads), lambda b, p, i: (b, p, i, 0)),
                  pl.BlockSpec((None, None, heads, seq), lambda b, p, i: (b, p, 0, 0))],
        out_specs=pl.BlockSpec((tq, LANES), lambda b, p, i: (b * nq + i, p)),
        out_shape=jax.ShapeDtypeStruct((batch * seq, n_groups * LANES), F32),
        scratch_shapes=[pltpu.VMEM((seq, LANES), BF16), pltpu.VMEM((heads, seq, LANES), BF16)],
        compiler_params=_params(("parallel", "parallel", "arbitrary")),
        name="fox_prefill",
    )(proj, proj, proj, cq, ck)


def _mlstm_chunk_kernel(q_ref, k_ref, v_ref, og_ref, irow_ref, frow_ref, ng_ref,
                        c0_ref, n0_ref, m0_ref, gated_ref, c_ref, n_ref, m_ref, *, chunk, n_heads, dk, dv):
    step = pl.program_id(1)

    @pl.when(step == 0)
    def _():
        c_ref[...] = c0_ref[...]
        n_ref[...] = n0_ref[...]
        m_ref[...] = m0_ref[...]

    t_idx = lax.broadcasted_iota(jnp.int32, (chunk, chunk), 0)
    s_idx = lax.broadcasted_iota(jnp.int32, (chunk, chunk), 1)
    causal = s_idx <= t_idx
    diagonal = s_idx == t_idx
    for h in range(n_heads):
        q = q_ref[:, h * dk:(h + 1) * dk]
        k = k_ref[:, h * dk:(h + 1) * dk] * dk ** -0.5
        v = v_ref[:, h * dv:(h + 1) * dv]
        qb, kb = q.astype(BF16), k.astype(BF16)
        b_col = jnp.sum(jnp.where(causal, frow_ref[h], 0.0), axis=1, keepdims=True)
        b_row = jnp.sum(jnp.where(diagonal, b_col, 0.0), axis=0, keepdims=True)
        i_col = jnp.sum(jnp.where(diagonal, irow_ref[h], 0.0), axis=1, keepdims=True)
        m_prev = m_ref[h]
        n_prev = n_ref[h]
        c_prev = c_ref[h]

        d = jnp.where(causal, b_col - b_row + irow_ref[h], -jnp.inf)
        g = b_col + m_prev
        m_t = jnp.maximum(g, jnp.max(d, axis=1, keepdims=True))
        wt = jnp.exp(d - m_t) * lax.dot_general(qb, kb, NT_DIMS, preferred_element_type=F32)
        decay = jnp.exp(g - m_t)
        num = (decay * lax.dot_general(qb, c_prev.astype(BF16), NT_DIMS, preferred_element_type=F32)
               + jnp.dot(wt.astype(BF16), v.astype(BF16), preferred_element_type=F32))
        den = decay * jnp.sum(q * n_prev, axis=1, keepdims=True) + jnp.sum(wt, axis=1, keepdims=True)
        hid = num / jnp.maximum(jnp.abs(den), jnp.exp(-m_t))

        hn = _rms(hid, ng_ref[h])
        gated_ref[:, h * dv:(h + 1) * dv] = (_sigmoid(og_ref[:, h * dv:(h + 1) * dv]) * hn).astype(gated_ref.dtype)

        b_last = b_col[chunk - 1:chunk, :]
        a_col = b_last - b_col + i_col
        m_new = jnp.maximum(b_last + m_prev, jnp.max(a_col, axis=0, keepdims=True))
        w_col = jnp.exp(a_col - m_new)
        carry_decay = jnp.exp(b_last + m_prev - m_new)
        wv_t = (w_col * v).T.astype(BF16)
        c_ref[h] = carry_decay * c_prev + jnp.dot(wv_t, kb, preferred_element_type=F32)
        n_ref[h] = carry_decay * n_prev + jnp.sum(w_col * k, axis=0, keepdims=True)
        m_ref[h] = m_new


def _mlstm_prefill(proj, gates_i, gates_f, norm_g, c0, n0, m0, batch, seq, n_heads, dk, dv):
    chunk = min(seq, MLSTM_CHUNK)
    nc = seq // chunk
    assert seq % chunk == 0 and dk == dv
    row4 = lambda a: a.reshape(batch, n_heads, 1, seq)
    tok = lambda group: pl.BlockSpec((chunk, n_heads * dk), lambda b, c: (b * nc + c, group))
    rowspec = pl.BlockSpec((None, n_heads, 1, chunk), lambda b, c: (b, 0, 0, c))
    cspec = pl.BlockSpec((None, n_heads, dv, dk), lambda b, c: (b, 0, 0, 0))
    nspec = pl.BlockSpec((None, n_heads, 1, dk), lambda b, c: (b, 0, 0, 0))
    mspec = pl.BlockSpec((None, n_heads, 1, 1), lambda b, c: (b, 0, 0, 0))
    return pl.pallas_call(
        functools.partial(_mlstm_chunk_kernel, chunk=chunk, n_heads=n_heads, dk=dk, dv=dv),
        grid=(batch, nc),
        in_specs=[tok(0), tok(1), tok(2), tok(3), rowspec, rowspec,
                  _resident((n_heads, 1, dv)), cspec, nspec, mspec],
        out_specs=[pl.BlockSpec((chunk, n_heads * dv), lambda b, c: (b * nc + c, 0)), cspec, nspec, mspec],
        out_shape=[jax.ShapeDtypeStruct((batch * seq, n_heads * dv), BF16),
                   jax.ShapeDtypeStruct((batch, n_heads, dv, dk), F32),
                   jax.ShapeDtypeStruct((batch, n_heads, 1, dk), F32),
                   jax.ShapeDtypeStruct((batch, n_heads, 1, 1), F32)],
        compiler_params=_params(("parallel", "arbitrary")),
        name="mlstm_chunk",
    )(proj, proj, proj, proj, row4(gates_i), row4(gates_f),
      norm_g.reshape(n_heads, 1, dv), c0, n0.reshape(batch, n_heads, 1, dk), m0.reshape(batch, n_heads, 1, 1))


def _block_diag_mask(n_heads, hd):
    r = lax.broadcasted_iota(jnp.int32, (n_heads, n_heads * hd), 0)
    lane = lax.broadcasted_iota(jnp.int32, (n_heads, n_heads * hd), 1)
    return (lane >= r * hd) & (lane < (r + 1) * hd)


def _lanes_to_col(row, n):
    r = lax.broadcasted_iota(jnp.int32, (n, row.shape[1]), 0)
    lane = lax.broadcasted_iota(jnp.int32, (n, row.shape[1]), 1)
    return jnp.sum(jnp.where(r == lane, row, 0.0), axis=1, keepdims=True)


def _decode_attn_kernel(pt_ref, row_ref, *refs, group, n_sb, hd_sb, n_fx, hd_fx, cols):
    del pt_ref
    ksb_refs, vsb_refs, kfx_refs, vfx_refs, lft_refs = (refs[g * group:(g + 1) * group] for g in range(5))
    o_ref, qsb_s, qfx_s, carry_sb_s, acc_sb_s, carry_fx_s, m_s, l_s, acc_fx_s = refs[5 * group:]
    p = pl.program_id(1)
    page = ksb_refs[0].shape[1]
    w_sb, w_fx = n_sb * hd_sb, n_fx * hd_fx
    q_sb_col, q_fx_col, k_fx_col, v_fx_col, gate_col = cols
    mask_sb = _block_diag_mask(n_sb, hd_sb)
    mask_fx = _block_diag_mask(n_fx, hd_fx)

    @pl.when(p == 0)
    def _():
        row = row_ref[...]
        q_sb = jnp.where(mask_sb, row[:, q_sb_col:q_sb_col + w_sb] * hd_sb ** -0.5, 0.0)
        q_fx = jnp.where(mask_fx, row[:, q_fx_col:q_fx_col + w_fx] * hd_fx ** -0.5, 0.0)
        qsb_s[...] = q_sb.astype(BF16)
        qfx_s[...] = q_fx.astype(BF16)
        carry_sb_s[...] = jnp.zeros_like(carry_sb_s)
        acc_sb_s[...] = jnp.zeros_like(acc_sb_s)
        k_cur = row[:, k_fx_col:k_fx_col + w_fx]
        v_cur = row[:, v_fx_col:v_fx_col + w_fx]
        m_s[...] = jnp.sum(q_fx * k_cur, axis=1, keepdims=True)
        l_s[...] = jnp.ones_like(l_s)
        acc_fx_s[...] = jnp.broadcast_to(v_cur, acc_fx_s.shape)
        carry_fx_s[...] = _lanes_to_col(row[:, gate_col:gate_col + LANES], n_fx)

    j = lax.broadcasted_iota(jnp.int32, (page, page), 0)
    s = lax.broadcasted_iota(jnp.int32, (page, page), 1)
    later_sel = (j > s).astype(F32)

    def lanes(refs, dtype):
        return jnp.concatenate([r[...].astype(dtype) for r in refs], axis=1)

    def later_bias(x, carry):
        parts = [x[:, g * page:(g + 1) * page] for g in range(group)]
        inside = jnp.dot(jnp.concatenate(parts, axis=0), later_sel, preferred_element_type=F32,
                         precision=lax.Precision.HIGHEST)
        rows = x.shape[0]
        out = []
        for g in range(group):
            out.append(inside[g * rows:(g + 1) * rows] + carry)
            carry = carry + jnp.sum(parts[g], axis=1, keepdims=True)
        return jnp.concatenate(out, axis=1), carry

    z = jnp.dot(qsb_s[...], lanes(ksb_refs, BF16), preferred_element_type=F32)
    log_beta = _log_sigmoid(z)
    later, carry_sb_s[...] = later_bias(log_beta - z, carry_sb_s[...])
    w = jnp.exp(log_beta + later)
    acc_sb_s[...] += lax.dot_general(w.astype(BF16), lanes(vsb_refs, BF16), NT_DIMS, preferred_element_type=F32)

    bias, carry_fx_s[...] = later_bias(lanes(lft_refs, F32), carry_fx_s[...])
    sc = jnp.dot(qfx_s[...], lanes(kfx_refs, BF16), preferred_element_type=F32) + bias
    m_new = jnp.maximum(m_s[...], jnp.max(sc, axis=1, keepdims=True))
    alpha = jnp.exp(m_s[...] - m_new)
    pr = jnp.exp(sc - m_new)
    l_s[...] = alpha * l_s[...] + jnp.sum(pr, axis=1, keepdims=True)
    acc_fx_s[...] = alpha * acc_fx_s[...] + lax.dot_general(
        pr.astype(BF16), lanes(vfx_refs, BF16), NT_DIMS, preferred_element_type=F32)
    m_s[...] = m_new

    @pl.when(p == pl.num_programs(1) - 1)
    def _():
        o_sb = jnp.sum(jnp.where(mask_sb, acc_sb_s[...], 0.0), axis=0, keepdims=True)
        o_fx = jnp.sum(jnp.where(mask_fx, acc_fx_s[...] / l_s[...], 0.0), axis=0, keepdims=True)
        o_ref[:, 0:w_sb] = o_sb
        o_ref[:, w_sb:w_sb + w_fx] = o_fx


def _decode_attn(proj, page_table, layer, cache_sb_k, cache_sb_v, cache_fox_k, cache_fox_v, cache_fox_logf, cols):
    b, n = proj.shape
    _, n_phys, page, n_sb, hd_sb = cache_sb_k.shape
    n_fx, hd_fx = cache_fox_k.shape[-2:]
    n_pages = page_table.shape[1]
    w_sb, w_fx = n_sb * hd_sb, n_fx * hd_fx
    flat = lambda a: jnp.transpose(a, (0, 1, 3, 4, 2)).reshape(a.shape[0], n_phys, -1, page)
    lft = jnp.swapaxes(cache_fox_logf, 2, 3)

    group = min(n_pages, DECODE_PAGES_PER_STEP)
    assert n_pages % group == 0

    def paged(rows, g):
        return pl.BlockSpec((None, None, rows, page),
                            lambda i, p, pt: (layer, pt[i, n_pages - 1 - p * group - g], 0, 0))

    caches = (flat(cache_sb_k), flat(cache_sb_v), flat(cache_fox_k), flat(cache_fox_v), lft)
    rows = (w_sb, w_sb, w_fx, w_fx, n_fx)
    grid_spec = pltpu.PrefetchScalarGridSpec(
        num_scalar_prefetch=1,
        grid=(b, n_pages // group),
        in_specs=[pl.BlockSpec((None, 1, n), lambda i, p, pt: (i, 0, 0))]
        + [paged(r, g) for r in rows for g in range(group)],
        out_specs=pl.BlockSpec((None, 1, w_sb + w_fx), lambda i, p, pt: (i, 0, 0)),
        scratch_shapes=[pltpu.VMEM((n_sb, w_sb), BF16), pltpu.VMEM((n_fx, w_fx), BF16),
                        pltpu.VMEM((n_sb, 1), F32), pltpu.VMEM((n_sb, w_sb), F32),
                        pltpu.VMEM((n_fx, 1), F32), pltpu.VMEM((n_fx, 1), F32), pltpu.VMEM((n_fx, 1), F32),
                        pltpu.VMEM((n_fx, w_fx), F32)])
    out = pl.pallas_call(
        functools.partial(_decode_attn_kernel, group=group, n_sb=n_sb, hd_sb=hd_sb, n_fx=n_fx, hd_fx=hd_fx,
                          cols=cols),
        grid_spec=grid_spec,
        out_shape=jax.ShapeDtypeStruct((b, 1, w_sb + w_fx), F32),
        compiler_params=_params(("parallel", "arbitrary")),
        name="decode_attn",
    )(page_table, proj.reshape(b, 1, n), *[c for c in caches for _ in range(group)])
    return out.reshape(b, w_sb + w_fx)


def _mlstm_step_kernel(row_ref, ng_ref, c0_ref, n0_ref, m0_ref, gated_ref, c_ref, n_ref, m_ref,
                       *, n_heads, dk, dv, gate_col):
    row = row_ref[...]
    gates = row[:, gate_col:gate_col + LANES]
    eye = (lax.broadcasted_iota(jnp.int32, (dv, dv), 0) == lax.broadcasted_iota(jnp.int32, (dv, dv), 1))
    for h in range(n_heads):
        q = row[:, h * dk:(h + 1) * dk]
        k = row[:, (n_heads + h) * dk:(n_heads + h + 1) * dk] * dk ** -0.5
        v = row[:, 2 * n_heads * dk + h * dv:2 * n_heads * dk + (h + 1) * dv]
        og = row[:, 2 * n_heads * dk + (n_heads + h) * dv:2 * n_heads * dk + (n_heads + h + 1) * dv]
        i_log = gates[:, h:h + 1]
        f_log = gates[:, n_heads + h:n_heads + h + 1]
        c_prev = c0_ref[h]
        n_prev = n0_ref[h:h + 1, :]
        m_prev = m0_ref[h:h + 1, :]
        g = f_log + m_prev
        m_t = jnp.maximum(g, i_log)
        w_in = jnp.exp(i_log - m_t)
        decay = jnp.exp(g - m_t)
        qk = jnp.sum(q * k, axis=1, keepdims=True)
        q8 = jnp.broadcast_to(q, (SUBLANES, dk)).astype(BF16)
        cq = lax.dot_general(q8, c_prev.astype(BF16), NT_DIMS, preferred_element_type=F32)[0:1, :]
        num = decay * cq + (w_in * qk) * v
        den = decay * jnp.sum(q * n_prev, axis=1, keepdims=True) + w_in * qk
        hid = num / jnp.maximum(jnp.abs(den), jnp.exp(-m_t))
        hn = _rms(hid, ng_ref[h:h + 1, :])
        gated_ref[:, h * dv:(h + 1) * dv] = _sigmoid(og) * hn
        v_col = jnp.sum(jnp.where(eye, v, 0.0), axis=1, keepdims=True)
        c_ref[h] = decay * c_prev + v_col * (w_in * k)
        n_ref[h:h + 1, :] = decay * n_prev + w_in * k
        m_ref[h:h + 1, :] = m_t


def _mlstm_decode(proj, norm_g, c0, n0, m0, gate_col):
    b, n = proj.shape
    _, n_heads, dv, dk = c0.shape
    seq_spec = lambda *tail: pl.BlockSpec((None,) + tail, lambda i: (i,) + (0,) * len(tail))
    gated, c, nn, m = pl.pallas_call(
        functools.partial(_mlstm_step_kernel, n_heads=n_heads, dk=dk, dv=dv, gate_col=gate_col),
        grid=(b,),
        in_specs=[seq_spec(1, n), _resident((n_heads, dv)), seq_spec(n_heads, dv, dk),
                  seq_spec(n_heads, dk), seq_spec(n_heads, 1)],
        out_specs=[seq_spec(1, n_heads * dv), seq_spec(n_heads, dv, dk), seq_spec(n_heads, dk),
                   seq_spec(n_heads, 1)],
        out_shape=[jax.ShapeDtypeStruct((b, 1, n_heads * dv), F32),
                   jax.ShapeDtypeStruct((b, n_heads, dv, dk), F32),
                   jax.ShapeDtypeStruct((b, n_heads, dk), F32),
                   jax.ShapeDtypeStruct((b, n_heads, 1), F32)],
        compiler_params=_params(("parallel",)),
        name="mlstm_step",
    )(proj.reshape(b, 1, n), norm_g, c0, n0, m0.reshape(b, n_heads, 1))
    return gated.reshape(b, n_heads * dv), c, nn, m.reshape(b, n_heads)


def _pad_gate_cols(w, n_main):
    n_gates = w.shape[1] - n_main
    return jnp.pad(w, ((0, 0), (0, LANES - n_gates))).astype(BF16)


def _gate_rows(bias, logsig_from):
    n = bias.shape[0]
    gb = jnp.pad(bias.astype(F32), (0, LANES - n)).reshape(1, LANES)
    gm = (jnp.arange(LANES) >= logsig_from) & (jnp.arange(LANES) < n)
    return gb, gm.astype(F32).reshape(1, LANES)


def _trunk(x, caches, page_table, ml_state, p):
    bsz, seq, d = x.shape
    m = bsz * seq
    x = x.reshape(m, d)
    depth = p["norm_g"].shape[0]
    n_sb, hd_sb, n_fx, hd_fx = p["n_sb"], p["hd_sb"], p["n_fx"], p["hd_fx"]
    w_sb, w_fx = n_sb * hd_sb, n_fx * hd_fx
    n_ml, dk, dv = p["n_ml"], p["dk"], p["dv"]
    att_rows, ml_rows = [], []
    for layer in range(depth):
        idx = layer // 2
        g = lambda j: p["norm_g"][layer, j].reshape(1, d)
        x = _ffn(x, g(0), p["ffn1_w_in"], p["ffn1_w_out"], g(1), layer)
        if layer % 2 == 0:
            gb, gm = _gate_rows(p["att_b_f"][idx], 0)
            main = 3 * w_sb + 3 * w_fx
            kv_cols = (w_sb, 2 * w_sb, 3 * w_sb + w_fx, 3 * w_sb + 2 * w_fx)
            kv_heads = ((n_sb, hd_sb), (n_sb, hd_sb), (n_fx, hd_fx), (n_fx, hd_fx))
            if caches is None:
                proj, *kv_t = _proj(x, g(2), p["att_w_in"][idx], gb, gm, seq, kv_cols)
                k_sb, v_sb, k_fx, v_fx = (
                    jnp.transpose(t.reshape(bsz, nh, hd, seq), (0, 3, 1, 2)) for t, (nh, hd) in zip(kv_t, kv_heads))
            else:
                proj, = _proj(x, g(2), p["att_w_in"][idx], gb, gm)
                k_sb, v_sb, k_fx, v_fx = (
                    proj[:, c:c + nh * hd].reshape(bsz, seq, nh, hd) for c, (nh, hd) in zip(kv_cols, kv_heads))
            lf = proj[:, main:main + n_fx].reshape(bsz, seq, n_fx)
            att_rows.append((k_sb, v_sb, k_fx, v_fx, lf))
            w_out = p["att_w_out"][idx]
            if caches is None:
                cb = lambda off: off // LANES
                o_sb = _sb_prefill(proj, bsz, seq, cb(0), cb(w_sb), cb(2 * w_sb), w_sb // LANES, hd_sb)
                c = _cumsum_lanes(jnp.swapaxes(lf, 1, 2))
                o_fx = _fox_prefill(proj, c, bsz, seq, cb(3 * w_sb), cb(3 * w_sb + w_fx),
                                    cb(3 * w_sb + 2 * w_fx), w_fx // LANES, hd_fx)
                mixer = ([o_sb, o_fx], [w_out[:w_sb], w_out[w_sb:]], g(3))
            else:
                assert seq == 1
                cols = (0, 3 * w_sb, 3 * w_sb + w_fx, 3 * w_sb + 2 * w_fx, main)
                o = _decode_attn(proj, page_table, idx, *caches, cols)
                mixer = ([o], [w_out], g(3))
        else:
            gb, gm = _gate_rows(jnp.concatenate([p["ml_b_i"][idx], p["ml_b_f"][idx]]), n_ml)
            proj, = _proj(x, g(2), p["ml_w_in"][idx], gb, gm)
            main = 2 * n_ml * dk + 2 * n_ml * dv
            if ml_state is None:
                gates = proj[:, main:main + 2 * n_ml].reshape(bsz, seq, 2 * n_ml)
                gates = jnp.transpose(gates, (0, 2, 1))
                zeros = lambda *s: jnp.zeros(s, F32)
                gated, c_new, n_new, m_new = _mlstm_prefill(
                    proj, gates[:, :n_ml], gates[:, n_ml:], p["ml_norm_g"][idx],
                    zeros(bsz, n_ml, dv, dk), zeros(bsz, n_ml, dk), zeros(bsz, n_ml), bsz, seq, n_ml, dk, dv)
                n_new = n_new.reshape(bsz, n_ml, dk)
                m_new = m_new.reshape(bsz, n_ml)
            else:
                assert seq == 1
                gated, c_new, n_new, m_new = _mlstm_decode(
                    proj, p["ml_norm_g"][idx], ml_state[0][idx], ml_state[1][idx], ml_state[2][idx], main)
            ml_rows.append((c_new, n_new, m_new))
            mixer = ([gated], [p["ml_w_out"][idx]], g(3))
        x = _ffn(x, g(4), p["ffn2_w_in"], p["ffn2_w_out"], g(5), layer, mixer)
    att_new = [jnp.stack([r[i] for r in att_rows]) for i in range(5)]
    ml_new = [jnp.stack([r[i] for r in ml_rows]) for i in range(3)]
    return x.reshape(bsz, seq, d), att_new, ml_new


def kernel(x_prompt, x_sample, cache_sb_k, cache_sb_v, cache_fox_k, cache_fox_v, cache_fox_logf,
           state_mlstm_C, state_mlstm_n, state_mlstm_m, page_table, norm_g, ffn1_w_in, ffn1_w_out,
           ffn2_w_in, ffn2_w_out, att_w_in, att_b_f, att_w_out, ml_w_in, ml_b_i, ml_b_f, ml_norm_g,
           ml_w_out):
    n_sb, hd_sb = cache_sb_k.shape[-2:]
    n_fx, hd_fx = cache_fox_k.shape[-2:]
    n_ml, dv, dk = state_mlstm_C.shape[-3:]
    att_main = 3 * n_sb * hd_sb + 3 * n_fx * hd_fx
    ml_main = 2 * n_ml * dk + 2 * n_ml * dv
    p = dict(
        norm_g=norm_g, n_sb=n_sb, hd_sb=hd_sb, n_fx=n_fx, hd_fx=hd_fx, n_ml=n_ml, dk=dk, dv=dv,
        ffn1_w_in=ffn1_w_in.astype(BF16), ffn1_w_out=ffn1_w_out.astype(BF16),
        ffn2_w_in=ffn2_w_in.astype(BF16), ffn2_w_out=ffn2_w_out.astype(BF16),
        att_w_in=jnp.stack([_pad_gate_cols(w, att_main) for w in att_w_in]), att_b_f=att_b_f,
        att_w_out=att_w_out.astype(BF16),
        ml_w_in=jnp.stack([_pad_gate_cols(w, ml_main) for w in ml_w_in]), ml_b_i=ml_b_i, ml_b_f=ml_b_f,
        ml_norm_g=ml_norm_g, ml_w_out=ml_w_out.astype(BF16))
    y_p, att_p, ml_p = _trunk(x_prompt, None, None, None, p)
    caches = (cache_sb_k, cache_sb_v, cache_fox_k, cache_fox_v, cache_fox_logf)
    y_s, att_s, ml_s = _trunk(x_sample, caches, page_table,
                              (state_mlstm_C, state_mlstm_n, state_mlstm_m), p)
    return (y_p, y_s, *att_p, *ml_p, *att_s, *ml_s)
```

```python
import functools

import jax
import jax.numpy as jnp
from jax import lax
from jax.experimental import pallas as pl
from jax.experimental.pallas import tpu as pltpu

NORM_EPS = 1e-6
LANES = 128
SUBLANES = 8
VMEM_LIMIT_BYTES = 56 * 1024 * 1024
MXU_DIM = 256
SB_TQ = 1024
FOX_TQ = 512
FOX_TK = 512
FOX_WIDE = 4
NEG_LOG2_E = -1.4426950408889634
MLSTM_CHUNK = 256
MASKED_MAX = -1e30
DECODE_PAGES_PER_STEP = 16

F32 = jnp.float32
BF16 = jnp.bfloat16
NT_DIMS = (((1,), (1,)), ((), ()))


def _rms(x, g):
    return x * lax.rsqrt(jnp.mean(x * x, axis=-1, keepdims=True) + NORM_EPS) * g


def _log_sigmoid(x):
    return jnp.minimum(x, 0.0) - jnp.log(1.0 + jnp.exp2(jnp.abs(x) * NEG_LOG2_E))


def _sigmoid(x):
    return 1.0 / (1.0 + jnp.exp(-x))


def _resident(shape):
    return pl.BlockSpec(shape, lambda *_: (0,) * len(shape), pipeline_mode=pl.Buffered(1))


def _params(semantics):
    return pltpu.CompilerParams(dimension_semantics=semantics, vmem_limit_bytes=VMEM_LIMIT_BYTES)


def _ffn_kernel(*refs, d_ff, tf, n_mix):
    x_ref, refs = refs[0], refs[1:]
    x = x_ref[...]
    if n_mix:
        a_refs, w_refs, gmix_ref = refs[:n_mix], refs[n_mix:2 * n_mix], refs[2 * n_mix]
        refs = refs[2 * n_mix + 1:]
        mix = None
        for a_ref, w_ref in zip(a_refs, w_refs):
            part = jnp.dot(a_ref[...].astype(BF16), w_ref[...], preferred_element_type=F32)
            mix = part if mix is None else mix + part
        x = x + _rms(mix, gmix_ref[...])
    gpre_ref, win_ref, wout_ref, gpost_ref, o_ref, acc_ref = refs
    h = _rms(x, gpre_ref[...]).astype(BF16)
    for c in range(d_ff // tf):
        gate = jnp.dot(h, win_ref[:, c * tf:(c + 1) * tf], preferred_element_type=F32)
        up = jnp.dot(h, win_ref[:, d_ff + c * tf:d_ff + (c + 1) * tf], preferred_element_type=F32)
        act = (gate * _sigmoid(gate) * up).astype(BF16)
        part = jnp.dot(act, wout_ref[c * tf:(c + 1) * tf, :], preferred_element_type=F32)
        if c == 0:
            acc_ref[...] = part
        else:
            acc_ref[...] += part
    o_ref[...] = x + 0.5 * _rms(acc_ref[...], gpost_ref[...])


def _ffn(x, g_pre, w_in, w_out, g_post, layer, mixer=None):
    m, d = x.shape
    d_ff = w_out.shape[1]
    tm = min(m, 512)
    tf = 256
    assert m % tm == 0 and d_ff % tf == 0
    of_layer = lambda *tail: pl.BlockSpec((None,) + tail, lambda i: (layer,) + (0,) * len(tail),
                                          pipeline_mode=pl.Buffered(1))
    acts, ws, g_mix = mixer if mixer else ((), (), None)
    mix_specs = [pl.BlockSpec((tm, a.shape[1]), lambda i: (i, 0)) for a in acts]
    mix_specs += [_resident(w.shape) for w in ws] + ([_resident((1, d))] if mixer else [])
    mix_args = [*acts, *ws] + ([g_mix] if mixer else [])
    return pl.pallas_call(
        functools.partial(_ffn_kernel, d_ff=d_ff, tf=tf, n_mix=len(acts)),
        grid=(m // tm,),
        in_specs=[pl.BlockSpec((tm, d), lambda i: (i, 0))] + mix_specs
        + [_resident((1, d)), of_layer(d, 2 * d_ff), of_layer(d_ff, d), _resident((1, d))],
        out_specs=pl.BlockSpec((tm, d), lambda i: (i, 0)),
        out_shape=jax.ShapeDtypeStruct((m, d), F32),
        scratch_shapes=[pltpu.VMEM((tm, d), F32)],
        compiler_params=_params(("parallel",)),
        name="ffn",
    )(x, *mix_args, g_pre, w_in, w_out, g_post)


def _proj_kernel(x_ref, g_ref, w_ref, gbias_ref, glogsig_ref, o_ref, *t_refs, n_main, chunk, t_cols):
    h = _rms(x_ref[...], g_ref[...]).astype(BF16)
    for c0 in range(0, n_main, chunk):
        c1 = min(c0 + chunk, n_main)
        y = jnp.dot(h, w_ref[:, c0:c1], preferred_element_type=F32)
        o_ref[:, c0:c1] = y
        if c0 in t_cols:
            t_refs[t_cols.index(c0)][...] = y.T
    y = jnp.dot(h, w_ref[:, n_main:n_main + LANES], preferred_element_type=F32) + gbias_ref[...]
    o_ref[:, n_main:n_main + LANES] = jnp.where(glogsig_ref[...] > 0.0, _log_sigmoid(y), y)


def _proj(x, g, w, gate_bias, gate_logsig, seq=None, t_cols=()):
    m, d = x.shape
    n = w.shape[1]
    n_main = n - LANES
    chunk = 512
    tm = min(m, 512)
    assert m % tm == 0 and all(c % chunk == 0 for c in t_cols)
    out_specs = [pl.BlockSpec((tm, n), lambda i: (i, 0))]
    out_shape = [jax.ShapeDtypeStruct((m, n), F32)]
    if t_cols:
        per_seq = seq // tm
        assert seq % tm == 0
        out_specs += [pl.BlockSpec((None, chunk, tm), lambda i: (i // per_seq, 0, i % per_seq))] * len(t_cols)
        out_shape += [jax.ShapeDtypeStruct((m // seq, chunk, seq), F32)] * len(t_cols)
    return pl.pallas_call(
        functools.partial(_proj_kernel, n_main=n_main, chunk=chunk, t_cols=tuple(t_cols)),
        grid=(m // tm,),
        in_specs=[pl.BlockSpec((tm, d), lambda i: (i, 0)),
                  _resident((1, d)), _resident((d, n)), _resident((1, LANES)), _resident((1, LANES))],
        out_specs=out_specs,
        out_shape=out_shape,
        compiler_params=_params(("parallel",)),
        name="proj",
    )(x, g, w, gate_bias, gate_logsig)


def _head_lane_mask(hd, h):
    lane = lax.broadcasted_iota(jnp.int32, (1, LANES), 1)
    return (lane >= h * hd) & (lane < (h + 1) * hd)


def _sb_prefill_kernel(q_ref, k_ref, v_ref, sel_ref, o_ref, kb_s, vh_s, *, tq, tk, hd, scale):
    i = pl.program_id(2)
    heads = LANES // hd
    per_tile = tq // tk
    seq = k_ref.shape[0]

    @pl.when(i == 0)
    def _():
        def convert(c, carry):
            rows = pl.ds(pl.multiple_of(c * tk, tk), tk)
            kb_s[rows, :] = k_ref[rows, :].astype(BF16)
            v = v_ref[rows, :]
            for h in range(heads):
                vh_s[h, rows, :] = jnp.where(_head_lane_mask(hd, h), v, 0.0).astype(BF16)
            return carry
        lax.fori_loop(0, seq // tk, convert, 0)

    q = q_ref[...] * scale
    qh = [jnp.where(_head_lane_mask(hd, h), q, 0.0).astype(BF16) for h in range(heads)]

    def block(j, r0, carries, acc, diagonal):
        n = tq - r0
        keys = pl.ds(pl.multiple_of(j * tk, tk), tk)
        kb = kb_s[keys, :]
        if diagonal:
            causal = (lax.broadcasted_iota(jnp.int32, (n, tk), 1)
                      < lax.broadcasted_iota(jnp.int32, (n, tk), 0))
        new_carries = []
        part = None
        for h in range(heads):
            carry = carries[h][r0:]
            z = lax.dot_general(qh[h][r0:], kb, NT_DIMS, preferred_element_type=F32)
            log_beta = _log_sigmoid(z)
            log_1m = log_beta - z
            if diagonal:
                log_1m = jnp.where(causal, log_1m, 0.0)
            later = jnp.dot(log_1m.astype(BF16), sel_ref[...], preferred_element_type=F32)
            w = jnp.exp(log_beta + later)
            if diagonal:
                w = jnp.where(causal, w, 0.0)
            pv = jnp.exp(carry) * jnp.dot(w.astype(BF16), vh_s[h, keys, :], preferred_element_type=F32)
            part = pv if part is None else part + pv
            carry = carry + (later[:, 0:1] + log_1m[:, 0:1])
            new_carries.append(carry if r0 == 0 else jnp.concatenate([carries[h][:r0], carry], axis=0))
        acc = acc + part if r0 == 0 else jnp.concatenate([acc[:r0], acc[r0:] + part], axis=0)
        return tuple(new_carries), acc

    carries = tuple(jnp.zeros((tq, 1), F32) for _ in range(heads))
    acc = jnp.zeros((tq, LANES), F32)
    for d in reversed(range(per_tile)):
        carries, acc = block(i * per_tile + d, d * tk, carries, acc, True)

    def body(t, state):
        for u in range(per_tile):
            state = block((i - t) * per_tile - 1 - u, 0, state[0], state[1], False)
        return state

    carries, acc = lax.fori_loop(0, i, body, (carries, acc))
    o_ref[...] = acc


def _sb_prefill(proj, batch, seq, q_col, k_col, v_col, n_groups, hd):
    tq = min(seq, SB_TQ)
    tk = min(seq, MXU_DIM)
    nq = seq // tq
    assert seq % tq == 0 and tq % tk == 0
    later_sel = jnp.tril(jnp.ones((tk, tk), BF16), -1)
    return pl.pallas_call(
        functools.partial(_sb_prefill_kernel, tq=tq, tk=tk, hd=hd, scale=hd ** -0.5),
        grid=(batch, n_groups, nq),
        in_specs=[pl.BlockSpec((tq, LANES), lambda b, p, i: (b * nq + i, q_col + p)),
                  pl.BlockSpec((seq, LANES), lambda b, p, i: (b, k_col + p)),
                  pl.BlockSpec((seq, LANES), lambda b, p, i: (b, v_col + p)),
                  _resident((tk, tk))],
        out_specs=pl.BlockSpec((tq, LANES), lambda b, p, i: (b * nq + i, p)),
        out_shape=jax.ShapeDtypeStruct((batch * seq, n_groups * LANES), F32),
        scratch_shapes=[pltpu.VMEM((seq, LANES), BF16), pltpu.VMEM((LANES // hd, seq, LANES), BF16)],
        compiler_params=_params(("parallel", "parallel", "arbitrary")),
        name="sb_prefill",
    )(proj, proj, proj, later_sel)


def _cumsum_kernel(x_ref, o_ref, *, chunk):
    rows, t = x_ref.shape
    j = lax.broadcasted_iota(jnp.int32, (chunk, chunk), 0)
    s = lax.broadcasted_iota(jnp.int32, (chunk, chunk), 1)
    incl = (j <= s).astype(F32)
    carry = jnp.zeros((rows, 1), F32)
    for c in range(t // chunk):
        x = x_ref[:, c * chunk:(c + 1) * chunk]
        o_ref[:, c * chunk:(c + 1) * chunk] = carry + jnp.dot(
            x, incl, preferred_element_type=F32, precision=lax.Precision.HIGHEST)
        carry = carry + jnp.sum(x, axis=-1, keepdims=True)


def _cumsum_lanes(x):
    b, r, t = x.shape
    chunk = min(t, MXU_DIM)
    assert t % chunk == 0
    return pl.pallas_call(
        functools.partial(_cumsum_kernel, chunk=chunk),
        grid=(b,),
        in_specs=[pl.BlockSpec((None, r, t), lambda i: (i, 0, 0))],
        out_specs=pl.BlockSpec((None, r, t), lambda i: (i, 0, 0)),
        out_shape=jax.ShapeDtypeStruct((b, r, t), F32),
        compiler_params=_params(("parallel",)),
        name="cumsum",
    )(x)


def _fox_prefill_kernel(q_ref, k_ref, v_ref, cq_ref, ck_ref, o_ref, kb_s, vh_s, *, tq, tk, hd, scale):
    i = pl.program_id(2)
    heads = LANES // hd
    per_tile = tq // tk
    seq = k_ref.shape[0]
    lane = lax.broadcasted_iota(jnp.int32, (1, LANES), 1)
    ones_lane = [((h + 1) % heads) * hd for h in range(heads)]

    @pl.when(i == 0)
    def _():
        def convert(c, carry):
            rows = pl.ds(pl.multiple_of(c * tk, tk), tk)
            kb_s[rows, :] = k_ref[rows, :].astype(BF16)
            v = v_ref[rows, :]
            for h in range(heads):
                vh = jnp.where(_head_lane_mask(hd, h), v, jnp.where(lane == ones_lane[h], 1.0, 0.0))
                vh_s[h, rows, :] = vh.astype(BF16)
            return carry
        lax.fori_loop(0, seq // tk, convert, 0)

    q = q_ref[...] * scale
    qh = [jnp.where(_head_lane_mask(hd, h), q, 0.0).astype(BF16) for h in range(heads)]
    cq = [cq_ref[:, h:h + 1] for h in range(heads)]

    def block(first, width, r0, ms, accs, diagonal):
        n = tq - r0
        nk = width * tk
        keys = pl.ds(pl.multiple_of(first * tk, nk), nk)
        kb = kb_s[keys, :]
        if diagonal:
            causal = (lax.broadcasted_iota(jnp.int32, (n, nk), 1)
                      <= lax.broadcasted_iota(jnp.int32, (n, nk), 0))
        new_ms, new_accs = [], []
        for h in range(heads):
            s = (lax.dot_general(qh[h][r0:], kb, NT_DIMS, preferred_element_type=F32)
                 + cq[h][r0:] - ck_ref[h:h + 1, keys])
            if diagonal:
                s = jnp.where(causal, s, -jnp.inf)
            m_old = ms[h][r0:]
            m_new = jnp.maximum(m_old, jnp.max(s, axis=-1, keepdims=True))
            p = jnp.exp(s - m_new)
            acc = (jnp.exp(m_old - m_new) * accs[h][r0:]
                   + jnp.dot(p.astype(BF16), vh_s[h, keys, :], preferred_element_type=F32))
            if r0:
                m_new = jnp.concatenate([ms[h][:r0], m_new], axis=0)
                acc = jnp.concatenate([accs[h][:r0], acc], axis=0)
            new_ms.append(m_new)
            new_accs.append(acc)
        return tuple(new_ms), tuple(new_accs)

    ms = tuple(jnp.full((tq, 1), MASKED_MAX, F32) for _ in range(heads))
    accs = tuple(jnp.zeros((tq, LANES), F32) for _ in range(heads))
    for d in reversed(range(per_tile)):
        ms, accs = block(i * per_tile + d, 1, d * tk, ms, accs, True)

    past = i * per_tile
    width, done = FOX_WIDE, 0
    while width >= 1:
        count = (past - done) // width if width == FOX_WIDE else ((past - done) // width) % 2
        base = done

        def span(t, state, width=width, base=base):
            return block(base + t * width, width, 0, state[0], state[1], False)

        ms, accs = lax.fori_loop(0, count, span, (ms, accs))
        done = done + count * width
        width //= 2
    out = None
    for h in range(heads):
        denom = accs[h][:, ones_lane[h]:ones_lane[h] + 1]
        part = jnp.where(_head_lane_mask(hd, h), accs[h] / denom, 0.0)
        out = part if out is None else out + part
    o_ref[...] = out


def _fox_prefill(proj, c, batch, seq, q_col, k_col, v_col, n_groups, hd):
    tq = min(seq, FOX_TQ)
    tk = min(seq, FOX_TK)
    nq = seq // tq
    heads = LANES // hd
    assert seq % tq == 0 and tq % tk == 0 and heads >= 2
    ck = c.reshape(batch, n_groups, heads, seq)
    cq = jnp.swapaxes(ck, 2, 3)
    return pl.pallas_call(
        functools.partial(_fox_prefill_kernel, tq=tq, tk=tk, hd=hd, scale=hd ** -0.5),
        grid=(batch, n_groups, nq),
        in_specs=[pl.BlockSpec((tq, LANES), lambda b, p, i: (b * nq + i, q_col + p)),
                  pl.BlockSpec((seq, LANES), lambda b, p, i: (b, k_col + p)),
                  pl.BlockSpec((seq, LANES), lambda b, p, i: (b, v_col + p)),
                  pl.BlockSpec((None, None, tq, heads), lambda b, p, i: (b, p, i, 0)),
                  pl.BlockSpec((None, None, heads, seq), lambda b, p, i: (b, p, 0, 0))],
        out_specs=pl.BlockSpec((tq, LANES), lambda b, p, i: (b * nq + i, p)),
        out_shape=jax.ShapeDtypeStruct((batch * seq, n_groups * LANES), F32),
        scratch_shapes=[pltpu.VMEM((seq, LANES), BF16), pltpu.VMEM((heads, seq, LANES), BF16)],
        compiler_params=_params(("parallel", "parallel", "arbitrary")),
        name="fox_prefill",
    )(proj, proj, proj, cq, ck)


def _mlstm_chunk_kernel(q_ref, k_ref, v_ref, og_ref, irow_ref, frow_ref, ng_ref,
                        c0_ref, n0_ref, m0_ref, gated_ref, c_ref, n_ref, m_ref, *, chunk, n_heads, dk, dv):
    step = pl.program_id(1)

    @pl.when(step == 0)
    def _():
        c_ref[...] = c0_ref[...]
        n_ref[...] = n0_ref[...]
        m_ref[...] = m0_ref[...]

    t_idx = lax.broadcasted_iota(jnp.int32, (chunk, chunk), 0)
    s_idx = lax.broadcasted_iota(jnp.int32, (chunk, chunk), 1)
    causal = s_idx <= t_idx
    diagonal = s_idx == t_idx
    for h in range(n_heads):
        q = q_ref[:, h * dk:(h + 1) * dk]
        k = k_ref[:, h * dk:(h + 1) * dk] * dk ** -0.5
        v = v_ref[:, h * dv:(h + 1) * dv]
        qb, kb = q.astype(BF16), k.astype(BF16)
        b_col = jnp.sum(jnp.where(causal, frow_ref[h], 0.0), axis=1, keepdims=True)
        b_row = jnp.sum(jnp.where(diagonal, b_col, 0.0), axis=0, keepdims=True)
        i_col = jnp.sum(jnp.where(diagonal, irow_ref[h], 0.0), axis=1, keepdims=True)
        m_prev = m_ref[h]
        n_prev = n_ref[h]
        c_prev = c_ref[h]

        d = jnp.where(causal, b_col - b_row + irow_ref[h], -jnp.inf)
        g = b_col + m_prev
        m_t = jnp.maximum(g, jnp.max(d, axis=1, keepdims=True))
        wt = jnp.exp(d - m_t) * lax.dot_general(qb, kb, NT_DIMS, preferred_element_type=F32)
        decay = jnp.exp(g - m_t)
        num = (decay * lax.dot_general(qb, c_prev.astype(BF16), NT_DIMS, preferred_element_type=F32)
               + jnp.dot(wt.astype(BF16), v.astype(BF16), preferred_element_type=F32))
        den = decay * jnp.sum(q * n_prev, axis=1, keepdims=True) + jnp.sum(wt, axis=1, keepdims=True)
        hid = num / jnp.maximum(jnp.abs(den), jnp.exp(-m_t))

        hn = _rms(hid, ng_ref[h])
        gated_ref[:, h * dv:(h + 1) * dv] = (_sigmoid(og_ref[:, h * dv:(h + 1) * dv]) * hn).astype(gated_ref.dtype)

        b_last = b_col[chunk - 1:chunk, :]
        a_col = b_last - b_col + i_col
        m_new = jnp.maximum(b_last + m_prev, jnp.max(a_col, axis=0, keepdims=True))
        w_col = jnp.exp(a_col - m_new)
        carry_decay = jnp.exp(b_last + m_prev - m_new)
        wv_t = (w_col * v).T.astype(BF16)
        c_ref[h] = carry_decay * c_prev + jnp.dot(wv_t, kb, preferred_element_type=F32)
        n_ref[h] = carry_decay * n_prev + jnp.sum(w_col * k, axis=0, keepdims=True)
        m_ref[h] = m_new


def _mlstm_prefill(proj, gates_i, gates_f, norm_g, c0, n0, m0, batch, seq, n_heads, dk, dv):
    chunk = min(seq, MLSTM_CHUNK)
    nc = seq // chunk
    assert seq % chunk == 0 and dk == dv
    row4 = lambda a: a.reshape(batch, n_heads, 1, seq)
    tok = lambda group: pl.BlockSpec((chunk, n_heads * dk), lambda b, c: (b * nc + c, group))
    rowspec = pl.BlockSpec((None, n_heads, 1, chunk), lambda b, c: (b, 0, 0, c))
    cspec = pl.BlockSpec((None, n_heads, dv, dk), lambda b, c: (b, 0, 0, 0))
    nspec = pl.BlockSpec((None, n_heads, 1, dk), lambda b, c: (b, 0, 0, 0))
    mspec = pl.BlockSpec((None, n_heads, 1, 1), lambda b, c: (b, 0, 0, 0))
    return pl.pallas_call(
        functools.partial(_mlstm_chunk_kernel, chunk=chunk, n_heads=n_heads, dk=dk, dv=dv),
        grid=(batch, nc),
        in_specs=[tok(0), tok(1), tok(2), tok(3), rowspec, rowspec,
                  _resident((n_heads, 1, dv)), cspec, nspec, mspec],
        out_specs=[pl.BlockSpec((chunk, n_heads * dv), lambda b, c: (b * nc + c, 0)), cspec, nspec, mspec],
        out_shape=[jax.ShapeDtypeStruct((batch * seq, n_heads * dv), BF16),
                   jax.ShapeDtypeStruct((batch, n_heads, dv, dk), F32),
                   jax.ShapeDtypeStruct((batch, n_heads, 1, dk), F32),
                   jax.ShapeDtypeStruct((batch, n_heads, 1, 1), F32)],
        compiler_params=_params(("parallel", "arbitrary")),
        name="mlstm_chunk",
    )(proj, proj, proj, proj, row4(gates_i), row4(gates_f),
      norm_g.reshape(n_heads, 1, dv), c0, n0.reshape(batch, n_heads, 1, dk), m0.reshape(batch, n_heads, 1, 1))


def _block_diag_mask(n_heads, hd):
    r = lax.broadcasted_iota(jnp.int32, (n_heads, n_heads * hd), 0)
    lane = lax.broadcasted_iota(jnp.int32, (n_heads, n_heads * hd), 1)
    return (lane >= r * hd) & (lane < (r + 1) * hd)


def _lanes_to_col(row, n):
    r = lax.broadcasted_iota(jnp.int32, (n, row.shape[1]), 0)
    lane = lax.broadcasted_iota(jnp.int32, (n, row.shape[1]), 1)
    return jnp.sum(jnp.where(r == lane, row, 0.0), axis=1, keepdims=True)


def _decode_attn_kernel(pt_ref, row_ref, *refs, group, n_sb, hd_sb, n_fx, hd_fx, cols):
    del pt_ref
    ksb_refs, vsb_refs, kfx_refs, vfx_refs, lft_refs = (refs[g * group:(g + 1) * group] for g in range(5))
    o_ref, qsb_s, qfx_s, carry_sb_s, acc_sb_s, carry_fx_s, m_s, l_s, acc_fx_s = refs[5 * group:]
    p = pl.program_id(1)
    page = ksb_refs[0].shape[1]
    w_sb, w_fx = n_sb * hd_sb, n_fx * hd_fx
    q_sb_col, q_fx_col, k_fx_col, v_fx_col, gate_col = cols
    mask_sb = _block_diag_mask(n_sb, hd_sb)
    mask_fx = _block_diag_mask(n_fx, hd_fx)

    @pl.when(p == 0)
    def _():
        row = row_ref[...]
        q_sb = jnp.where(mask_sb, row[:, q_sb_col:q_sb_col + w_sb] * hd_sb ** -0.5, 0.0)
        q_fx = jnp.where(mask_fx, row[:, q_fx_col:q_fx_col + w_fx] * hd_fx ** -0.5, 0.0)
        qsb_s[...] = q_sb.astype(BF16)
        qfx_s[...] = q_fx.astype(BF16)
        carry_sb_s[...] = jnp.zeros_like(carry_sb_s)
        acc_sb_s[...] = jnp.zeros_like(acc_sb_s)
        k_cur = row[:, k_fx_col:k_fx_col + w_fx]
        v_cur = row[:, v_fx_col:v_fx_col + w_fx]
        m_s[...] = jnp.sum(q_fx * k_cur, axis=1, keepdims=True)
        l_s[...] = jnp.ones_like(l_s)
        acc_fx_s[...] = jnp.broadcast_to(v_cur, acc_fx_s.shape)
        carry_fx_s[...] = _lanes_to_col(row[:, gate_col:gate_col + LANES], n_fx)

    j = lax.broadcasted_iota(jnp.int32, (page, page), 0)
    s = lax.broadcasted_iota(jnp.int32, (page, page), 1)
    later_sel = (j > s).astype(F32)

    def lanes(refs, dtype):
        return jnp.concatenate([r[...].astype(dtype) for r in refs], axis=1)

    def later_bias(x, carry):
        parts = [x[:, g * page:(g + 1) * page] for g in range(group)]
        inside = jnp.dot(jnp.concatenate(parts, axis=0), later_sel, preferred_element_type=F32,
                         precision=lax.Precision.HIGHEST)
        rows = x.shape[0]
        out = []
        for g in range(group):
            out.append(inside[g * rows:(g + 1) * rows] + carry)
            carry = carry + jnp.sum(parts[g], axis=1, keepdims=True)
        return jnp.concatenate(out, axis=1), carry

    z = jnp.dot(qsb_s[...], lanes(ksb_refs, BF16), preferred_element_type=F32)
    log_beta = _log_sigmoid(z)
    later, carry_sb_s[...] = later_bias(log_beta - z, carry_sb_s[...])
    w = jnp.exp(log_beta + later)
    acc_sb_s[...] += lax.dot_general(w.astype(BF16), lanes(vsb_refs, BF16), NT_DIMS, preferred_element_type=F32)

    bias, carry_fx_s[...] = later_bias(lanes(lft_refs, F32), carry_fx_s[...])
    sc = jnp.dot(qfx_s[...], lanes(kfx_refs, BF16), preferred_element_type=F32) + bias
    m_new = jnp.maximum(m_s[...], jnp.max(sc, axis=1, keepdims=True))
    alpha = jnp.exp(m_s[...] - m_new)
    pr = jnp.exp(sc - m_new)
    l_s[...] = alpha * l_s[...] + jnp.sum(pr, axis=1, keepdims=True)
    acc_fx_s[...] = alpha * acc_fx_s[...] + lax.dot_general(
        pr.astype(BF16), lanes(vfx_refs, BF16), NT_DIMS, preferred_element_type=F32)
    m_s[...] = m_new

    @pl.when(p == pl.num_programs(1) - 1)
    def _():
        o_sb = jnp.sum(jnp.where(mask_sb, acc_sb_s[...], 0.0), axis=0, keepdims=True)
        o_fx = jnp.sum(jnp.where(mask_fx, acc_fx_s[...] / l_s[...], 0.0), axis=0, keepdims=True)
        o_ref[:, 0:w_sb] = o_sb
        o_ref[:, w_sb:w_sb + w_fx] = o_fx


def _decode_attn(proj, page_table, layer, cache_sb_k, cache_sb_v, cache_fox_k, cache_fox_v, cache_fox_logf, cols):
    b, n = proj.shape
    _, n_phys, page, n_sb, hd_sb = cache_sb_k.shape
    n_fx, hd_fx = cache_fox_k.shape[-2:]
    n_pages = page_table.shape[1]
    w_sb, w_fx = n_sb * hd_sb, n_fx * hd_fx
    flat = lambda a: jnp.transpose(a, (0, 1, 3, 4, 2)).reshape(a.shape[0], n_phys, -1, page)
    lft = jnp.swapaxes(cache_fox_logf, 2, 3)

    group = min(n_pages, DECODE_PAGES_PER_STEP)
    assert n_pages % group == 0

    def paged(rows, g):
        return pl.BlockSpec((None, None, rows, page),
                            lambda i, p, pt: (layer, pt[i, n_pages - 1 - p * group - g], 0, 0))

    caches = (flat(cache_sb_k), flat(cache_sb_v), flat(cache_fox_k), flat(cache_fox_v), lft)
    rows = (w_sb, w_sb, w_fx, w_fx, n_fx)
    grid_spec = pltpu.PrefetchScalarGridSpec(
        num_scalar_prefetch=1,
        grid=(b, n_pages // group),
        in_specs=[pl.BlockSpec((None, 1, n), lambda i, p, pt: (i, 0, 0))]
        + [paged(r, g) for r in rows for g in range(group)],
        out_specs=pl.BlockSpec((None, 1, w_sb + w_fx), lambda i, p, pt: (i, 0, 0)),
        scratch_shapes=[pltpu.VMEM((n_sb, w_sb), BF16), pltpu.VMEM((n_fx, w_fx), BF16),
                        pltpu.VMEM((n_sb, 1), F32), pltpu.VMEM((n_sb, w_sb), F32),
                        pltpu.VMEM((n_fx, 1), F32), pltpu.VMEM((n_fx, 1), F32), pltpu.VMEM((n_fx, 1), F32),
                        pltpu.VMEM((n_fx, w_fx), F32)])
    out = pl.pallas_call(
        functools.partial(_decode_attn_kernel, group=group, n_sb=n_sb, hd_sb=hd_sb, n_fx=n_fx, hd_fx=hd_fx,
                          cols=cols),
        grid_spec=grid_spec,
        out_shape=jax.ShapeDtypeStruct((b, 1, w_sb + w_fx), F32),
        compiler_params=_params(("parallel", "arbitrary")),
        name="decode_attn",
    )(page_table, proj.reshape(b, 1, n), *[c for c in caches for _ in range(group)])
    return out.reshape(b, w_sb + w_fx)


def _mlstm_step_kernel(row_ref, ng_ref, c0_ref, n0_ref, m0_ref, gated_ref, c_ref, n_ref, m_ref,
                       *, n_heads, dk, dv, gate_col):
    row = row_ref[...]
    gates = row[:, gate_col:gate_col + LANES]
    eye = (lax.broadcasted_iota(jnp.int32, (dv, dv), 0) == lax.broadcasted_iota(jnp.int32, (dv, dv), 1))
    for h in range(n_heads):
        q = row[:, h * dk:(h + 1) * dk]
        k = row[:, (n_heads + h) * dk:(n_heads + h + 1) * dk] * dk ** -0.5
        v = row[:, 2 * n_heads * dk + h * dv:2 * n_heads * dk + (h + 1) * dv]
        og = row[:, 2 * n_heads * dk + (n_heads + h) * dv:2 * n_heads * dk + (n_heads + h + 1) * dv]
        i_log = gates[:, h:h + 1]
        f_log = gates[:, n_heads + h:n_heads + h + 1]
        c_prev = c0_ref[h]
        n_prev = n0_ref[h:h + 1, :]
        m_prev = m0_ref[h:h + 1, :]
        g = f_log + m_prev
        m_t = jnp.maximum(g, i_log)
        w_in = jnp.exp(i_log - m_t)
        decay = jnp.exp(g - m_t)
        qk = jnp.sum(q * k, axis=1, keepdims=True)
        q8 = jnp.broadcast_to(q, (SUBLANES, dk)).astype(BF16)
        cq = lax.dot_general(q8, c_prev.astype(BF16), NT_DIMS, preferred_element_type=F32)[0:1, :]
        num = decay * cq + (w_in * qk) * v
        den = decay * jnp.sum(q * n_prev, axis=1, keepdims=True) + w_in * qk
        hid = num / jnp.maximum(jnp.abs(den), jnp.exp(-m_t))
        hn = _rms(hid, ng_ref[h:h + 1, :])
        gated_ref[:, h * dv:(h + 1) * dv] = _sigmoid(og) * hn
        v_col = jnp.sum(jnp.where(eye, v, 0.0), axis=1, keepdims=True)
        c_ref[h] = decay * c_prev + v_col * (w_in * k)
        n_ref[h:h + 1, :] = decay * n_prev + w_in * k
        m_ref[h:h + 1, :] = m_t


def _mlstm_decode(proj, norm_g, c0, n0, m0, gate_col):
    b, n = proj.shape
    _, n_heads, dv, dk = c0.shape
    seq_spec = lambda *tail: pl.BlockSpec((None,) + tail, lambda i: (i,) + (0,) * len(tail))
    gated, c, nn, m = pl.pallas_call(
        functools.partial(_mlstm_step_kernel, n_heads=n_heads, dk=dk, dv=dv, gate_col=gate_col),
        grid=(b,),
        in_specs=[seq_spec(1, n), _resident((n_heads, dv)), seq_spec(n_heads, dv, dk),
                  seq_spec(n_heads, dk), seq_spec(n_heads, 1)],
        out_specs=[seq_spec(1, n_heads * dv), seq_spec(n_heads, dv, dk), seq_spec(n_heads, dk),
                   seq_spec(n_heads, 1)],
        out_shape=[jax.ShapeDtypeStruct((b, 1, n_heads * dv), F32),
                   jax.ShapeDtypeStruct((b, n_heads, dv, dk), F32),
                   jax.ShapeDtypeStruct((b, n_heads, dk), F32),
                   jax.ShapeDtypeStruct((b, n_heads, 1), F32)],
        compiler_params=_params(("parallel",)),
        name="mlstm_step",
    )(proj.reshape(b, 1, n), norm_g, c0, n0, m0.reshape(b, n_heads, 1))
    return gated.reshape(b, n_heads * dv), c, nn, m.reshape(b, n_heads)


def _pad_gate_cols(w, n_main):
    n_gates = w.shape[1] - n_main
    return jnp.pad(w, ((0, 0), (0, LANES - n_gates))).astype(BF16)


def _gate_rows(bias, logsig_from):
    n = bias.shape[0]
    gb = jnp.pad(bias.astype(F32), (0, LANES - n)).reshape(1, LANES)
    gm = (jnp.arange(LANES) >= logsig_from) & (jnp.arange(LANES) < n)
    return gb, gm.astype(F32).reshape(1, LANES)


def _trunk(x, caches, page_table, ml_state, p):
    bsz, seq, d = x.shape
    m = bsz * seq
    x = x.reshape(m, d)
    depth = p["norm_g"].shape[0]
    n_sb, hd_sb, n_fx, hd_fx = p["n_sb"], p["hd_sb"], p["n_fx"], p["hd_fx"]
    w_sb, w_fx = n_sb * hd_sb, n_fx * hd_fx
    n_ml, dk, dv = p["n_ml"], p["dk"], p["dv"]
    att_rows, ml_rows = [], []
    for layer in range(depth):
        idx = layer // 2
        g = lambda j: p["norm_g"][layer, j].reshape(1, d)
        x = _ffn(x, g(0), p["ffn1_w_in"], p["ffn1_w_out"], g(1), layer)
        if layer % 2 == 0:
            gb, gm = _gate_rows(p["att_b_f"][idx], 0)
            main = 3 * w_sb + 3 * w_fx
            kv_cols = (w_sb, 2 * w_sb, 3 * w_sb + w_fx, 3 * w_sb + 2 * w_fx)
            kv_heads = ((n_sb, hd_sb), (n_sb, hd_sb), (n_fx, hd_fx), (n_fx, hd_fx))
            if caches is None:
                proj, *kv_t = _proj(x, g(2), p["att_w_in"][idx], gb, gm, seq, kv_cols)
                k_sb, v_sb, k_fx, v_fx = (
                    jnp.transpose(t.reshape(bsz, nh, hd, seq), (0, 3, 1, 2)) for t, (nh, hd) in zip(kv_t, kv_heads))
            else:
                proj, = _proj(x, g(2), p["att_w_in"][idx], gb, gm)
                k_sb, v_sb, k_fx, v_fx = (
                    proj[:, c:c + nh * hd].reshape(bsz, seq, nh, hd) for c, (nh, hd) in zip(kv_cols, kv_heads))
            lf = proj[:, main:main + n_fx].reshape(bsz, seq, n_fx)
            att_rows.append((k_sb, v_sb, k_fx, v_fx, lf))
            w_out = p["att_w_out"][idx]
            if caches is None:
                cb = lambda off: off // LANES
                o_sb = _sb_prefill(proj, bsz, seq, cb(0), cb(w_sb), cb(2 * w_sb), w_sb // LANES, hd_sb)
                c = _cumsum_lanes(jnp.swapaxes(lf, 1, 2))
                o_fx = _fox_prefill(proj, c, bsz, seq, cb(3 * w_sb), cb(3 * w_sb + w_fx),
                                    cb(3 * w_sb + 2 * w_fx), w_fx // LANES, hd_fx)
                mixer = ([o_sb, o_fx], [w_out[:w_sb], w_out[w_sb:]], g(3))
            else:
                assert seq == 1
                cols = (0, 3 * w_sb, 3 * w_sb + w_fx, 3 * w_sb + 2 * w_fx, main)
                o = _decode_attn(proj, page_table, idx, *caches, cols)
                mixer = ([o], [w_out], g(3))
        else:
            gb, gm = _gate_rows(jnp.concatenate([p["ml_b_i"][idx], p["ml_b_f"][idx]]), n_ml)
            proj, = _proj(x, g(2), p["ml_w_in"][idx], gb, gm)
            main = 2 * n_ml * dk + 2 * n_ml * dv
            if ml_state is None:
                gates = proj[:, main:main + 2 * n_ml].reshape(bsz, seq, 2 * n_ml)
                gates = jnp.transpose(gates, (0, 2, 1))
                zeros = lambda *s: jnp.zeros(s, F32)
                gated, c_new, n_new, m_new = _mlstm_prefill(
                    proj, gates[:, :n_ml], gates[:, n_ml:], p["ml_norm_g"][idx],
                    zeros(bsz, n_ml, dv, dk), zeros(bsz, n_ml, dk), zeros(bsz, n_ml), bsz, seq, n_ml, dk, dv)
                n_new = n_new.reshape(bsz, n_ml, dk)
                m_new = m_new.reshape(bsz, n_ml)
            else:
                assert seq == 1
                gated, c_new, n_new, m_new = _mlstm_decode(
                    proj, p["ml_norm_g"][idx], ml_state[0][idx], ml_state[1][idx], ml_state[2][idx], main)
            ml_rows.append((c_new, n_new, m_new))
            mixer = ([gated], [p["ml_w_out"][idx]], g(3))
        x = _ffn(x, g(4), p["ffn2_w_in"], p["ffn2_w_out"], g(5), layer, mixer)
    att_new = [jnp.stack([r[i] for r in att_rows]) for i in range(5)]
    ml_new = [jnp.stack([r[i] for r in ml_rows]) for i in range(3)]
    return x.reshape(bsz, seq, d), att_new, ml_new


def kernel(x_prompt, x_sample, cache_sb_k, cache_sb_v, cache_fox_k, cache_fox_v, cache_fox_logf,
           state_mlstm_C, state_mlstm_n, state_mlstm_m, page_table, norm_g, ffn1_w_in, ffn1_w_out,
           ffn2_w_in, ffn2_w_out, att_w_in, att_b_f, att_w_out, ml_w_in, ml_b_i, ml_b_f, ml_norm_g,
           ml_w_out):
    n_sb, hd_sb = cache_sb_k.shape[-2:]
    n_fx, hd_fx = cache_fox_k.shape[-2:]
    n_ml, dv, dk = state_mlstm_C.shape[-3:]
    att_main = 3 * n_sb * hd_sb + 3 * n_fx * hd_fx
    ml_main = 2 * n_ml * dk + 2 * n_ml * dv
    p = dict(
        norm_g=norm_g, n_sb=n_sb, hd_sb=hd_sb, n_fx=n_fx, hd_fx=hd_fx, n_ml=n_ml, dk=dk, dv=dv,
        ffn1_w_in=ffn1_w_in.astype(BF16), ffn1_w_out=ffn1_w_out.astype(BF16),
        ffn2_w_in=ffn2_w_in.astype(BF16), ffn2_w_out=ffn2_w_out.astype(BF16),
        att_w_in=jnp.stack([_pad_gate_cols(w, att_main) for w in att_w_in]), att_b_f=att_b_f,
        att_w_out=att_w_out.astype(BF16),
        ml_w_in=jnp.stack([_pad_gate_cols(w, ml_main) for w in ml_w_in]), ml_b_i=ml_b_i, ml_b_f=ml_b_f,
        ml_norm_g=ml_norm_g, ml_w_out=ml_w_out.astype(BF16))
    y_p, att_p, ml_p = _trunk(x_prompt, None, None, None, p)
    caches = (cache_sb_k, cache_sb_v, cache_fox_k, cache_fox_v, cache_fox_logf)
    y_s, att_s, ml_s = _trunk(x_sample, caches, page_table,
                              (state_mlstm_C, state_mlstm_n, state_mlstm_m), p)
    return (y_p, y_s, *att_p, *ml_p, *att_s, *ml_s)
```

```python
import functools

import jax
import jax.numpy as jnp
from jax import lax
from jax.experimental import pallas as pl
from jax.experimental.pallas import tpu as pltpu

NORM_EPS = 1e-6
LANES = 128
SUBLANES = 8
VMEM_LIMIT_BYTES = 56 * 1024 * 1024
MXU_DIM = 256
SB_TQ = 1024
FOX_TQ = 512
FOX_TK = 512
FOX_WIDE = 4
NEG_LOG2_E = -1.4426950408889634
MLSTM_CHUNK = 256
MASKED_MAX = -1e30
DECODE_PAGES_PER_STEP = 16

F32 = jnp.float32
BF16 = jnp.bfloat16
NT_DIMS = (((1,), (1,)), ((), ()))


def _rms(x, g):
    return x * lax.rsqrt(jnp.mean(x * x, axis=-1, keepdims=True) + NORM_EPS) * g


def _log_sigmoid(x):
    return jnp.minimum(x, 0.0) - jnp.log(1.0 + jnp.exp2(jnp.abs(x) * NEG_LOG2_E))


def _sigmoid(x):
    return 1.0 / (1.0 + jnp.exp(-x))


def _resident(shape):
    return pl.BlockSpec(shape, lambda *_: (0,) * len(shape), pipeline_mode=pl.Buffered(1))


def _params(semantics):
    return pltpu.CompilerParams(dimension_semantics=semantics, vmem_limit_bytes=VMEM_LIMIT_BYTES)


def _ffn_kernel(*refs, d_ff, tf, n_mix):
    x_ref, refs = refs[0], refs[1:]
    x = x_ref[...]
    if n_mix:
        a_refs, w_refs, gmix_ref = refs[:n_mix], refs[n_mix:2 * n_mix], refs[2 * n_mix]
        refs = refs[2 * n_mix + 1:]
        mix = None
        for a_ref, w_ref in zip(a_refs, w_refs):
            part = jnp.dot(a_ref[...].astype(BF16), w_ref[...], preferred_element_type=F32)
            mix = part if mix is None else mix + part
        x = x + _rms(mix, gmix_ref[...])
    gpre_ref, win_ref, wout_ref, gpost_ref, o_ref, acc_ref = refs
    h = _rms(x, gpre_ref[...]).astype(BF16)
    for c in range(d_ff // tf):
        gate = jnp.dot(h, win_ref[:, c * tf:(c + 1) * tf], preferred_element_type=F32)
        up = jnp.dot(h, win_ref[:, d_ff + c * tf:d_ff + (c + 1) * tf], preferred_element_type=F32)
        act = (gate * _sigmoid(gate) * up).astype(BF16)
        part = jnp.dot(act, wout_ref[c * tf:(c + 1) * tf, :], preferred_element_type=F32)
        if c == 0:
            acc_ref[...] = part
        else:
            acc_ref[...] += part
    o_ref[...] = x + 0.5 * _rms(acc_ref[...], gpost_ref[...])


def _ffn(x, g_pre, w_in, w_out, g_post, layer, mixer=None):
    m, d = x.shape
    d_ff = w_out.shape[1]
    tm = min(m, 512)
    tf = 256
    assert m % tm == 0 and d_ff % tf == 0
    of_layer = lambda *tail: pl.BlockSpec((None,) + tail, lambda i: (layer,) + (0,) * len(tail),
                                          pipeline_mode=pl.Buffered(1))
    acts, ws, g_mix = mixer if mixer else ((), (), None)
    mix_specs = [pl.BlockSpec((tm, a.shape[1]), lambda i: (i, 0)) for a in acts]
    mix_specs += [_resident(w.shape) for w in ws] + ([_resident((1, d))] if mixer else [])
    mix_args = [*acts, *ws] + ([g_mix] if mixer else [])
    return pl.pallas_call(
        functools.partial(_ffn_kernel, d_ff=d_ff, tf=tf, n_mix=len(acts)),
        grid=(m // tm,),
        in_specs=[pl.BlockSpec((tm, d), lambda i: (i, 0))] + mix_specs
        + [_resident((1, d)), of_layer(d, 2 * d_ff), of_layer(d_ff, d), _resident((1, d))],
        out_specs=pl.BlockSpec((tm, d), lambda i: (i, 0)),
        out_shape=jax.ShapeDtypeStruct((m, d), F32),
        scratch_shapes=[pltpu.VMEM((tm, d), F32)],
        compiler_params=_params(("parallel",)),
        name="ffn",
    )(x, *mix_args, g_pre, w_in, w_out, g_post)


def _proj_kernel(x_ref, g_ref, w_ref, gbias_ref, glogsig_ref, o_ref, *t_refs, n_main, chunk, t_cols):
    h = _rms(x_ref[...], g_ref[...]).astype(BF16)
    for c0 in range(0, n_main, chunk):
        c1 = min(c0 + chunk, n_main)
        y = jnp.dot(h, w_ref[:, c0:c1], preferred_element_type=F32)
        o_ref[:, c0:c1] = y
        if c0 in t_cols:
            t_refs[t_cols.index(c0)][...] = y.T
    y = jnp.dot(h, w_ref[:, n_main:n_main + LANES], preferred_element_type=F32) + gbias_ref[...]
    o_ref[:, n_main:n_main + LANES] = jnp.where(glogsig_ref[...] > 0.0, _log_sigmoid(y), y)


def _proj(x, g, w, gate_bias, gate_logsig, seq=None, t_cols=()):
    m, d = x.shape
    n = w.shape[1]
    n_main = n - LANES
    chunk = 512
    tm = min(m, 512)
    assert m % tm == 0 and all(c % chunk == 0 for c in t_cols)
    out_specs = [pl.BlockSpec((tm, n), lambda i: (i, 0))]
    out_shape = [jax.ShapeDtypeStruct((m, n), F32)]
    if t_cols:
        per_seq = seq // tm
        assert seq % tm == 0
        out_specs += [pl.BlockSpec((None, chunk, tm), lambda i: (i // per_seq, 0, i % per_seq))] * len(t_cols)
        out_shape += [jax.ShapeDtypeStruct((m // seq, chunk, seq), F32)] * len(t_cols)
    return pl.pallas_call(
        functools.partial(_proj_kernel, n_main=n_main, chunk=chunk, t_cols=tuple(t_cols)),
        grid=(m // tm,),
        in_specs=[pl.BlockSpec((tm, d), lambda i: (i, 0)),
                  _resident((1, d)), _resident((d, n)), _resident((1, LANES)), _resident((1, LANES))],
        out_specs=out_specs,
        out_shape=out_shape,
        compiler_params=_params(("parallel",)),
        name="proj",
    )(x, g, w, gate_bias, gate_logsig)


def _head_lane_mask(hd, h):
    lane = lax.broadcasted_iota(jnp.int32, (1, LANES), 1)
    return (lane >= h * hd) & (lane < (h + 1) * hd)


def _sb_prefill_kernel(q_ref, k_ref, v_ref, sel_ref, o_ref, kb_s, vh_s, *, tq, tk, hd, scale):
    i = pl.program_id(2)
    heads = LANES // hd
    per_tile = tq // tk
    seq = k_ref.shape[0]

    @pl.when(i == 0)
    def _():
        def convert(c, carry):
            rows = pl.ds(pl.multiple_of(c * tk, tk), tk)
            kb_s[rows, :] = k_ref[rows, :].astype(BF16)
            v = v_ref[rows, :]
            for h in range(heads):
                vh_s[h, rows, :] = jnp.where(_head_lane_mask(hd, h), v, 0.0).astype(BF16)
            return carry
        lax.fori_loop(0, seq // tk, convert, 0)

    q = q_ref[...] * scale
    qh = [jnp.where(_head_lane_mask(hd, h), q, 0.0).astype(BF16) for h in range(heads)]

    def block(j, r0, carries, acc, diagonal):
        n = tq - r0
        keys = pl.ds(pl.multiple_of(j * tk, tk), tk)
        kb = kb_s[keys, :]
        if diagonal:
            causal = (lax.broadcasted_iota(jnp.int32, (n, tk), 1)
                      < lax.broadcasted_iota(jnp.int32, (n, tk), 0))
        new_carries = []
        part = None
        for h in range(heads):
            carry = carries[h][r0:]
            z = lax.dot_general(qh[h][r0:], kb, NT_DIMS, preferred_element_type=F32)
            softplus = jnp.maximum(z, 0.0) + jnp.log(1.0 + jnp.exp2(jnp.abs(z) * NEG_LOG2_E))
            if diagonal:
                softplus = jnp.where(causal, softplus, 0.0)
            later = jnp.dot(softplus.astype(BF16), sel_ref[...], preferred_element_type=F32)
            w = jnp.exp(z - softplus + later)
            if diagonal:
                w = jnp.where(causal, w, 0.0)
            pv = jnp.exp(carry) * jnp.dot(w.astype(BF16), vh_s[h, keys, :], preferred_element_type=F32)
            part = pv if part is None else part + pv
            carry = carry + (later[:, 0:1] - softplus[:, 0:1])
            new_carries.append(carry if r0 == 0 else jnp.concatenate([carries[h][:r0], carry], axis=0))
        acc = acc + part if r0 == 0 else jnp.concatenate([acc[:r0], acc[r0:] + part], axis=0)
        return tuple(new_carries), acc

    carries = tuple(jnp.zeros((tq, 1), F32) for _ in range(heads))
    acc = jnp.zeros((tq, LANES), F32)
    for d in reversed(range(per_tile)):
        carries, acc = block(i * per_tile + d, d * tk, carries, acc, True)

    def body(t, state):
        for u in range(per_tile):
            state = block((i - t) * per_tile - 1 - u, 0, state[0], state[1], False)
        return state

    carries, acc = lax.fori_loop(0, i, body, (carries, acc))
    o_ref[...] = acc


def _sb_prefill(proj, batch, seq, q_col, k_col, v_col, n_groups, hd):
    tq = min(seq, SB_TQ)
    tk = min(seq, MXU_DIM)
    nq = seq // tq
    assert seq % tq == 0 and tq % tk == 0
    later_sel = -jnp.tril(jnp.ones((tk, tk), BF16), -1)
    return pl.pallas_call(
        functools.partial(_sb_prefill_kernel, tq=tq, tk=tk, hd=hd, scale=hd ** -0.5),
        grid=(batch, n_groups, nq),
        in_specs=[pl.BlockSpec((tq, LANES), lambda b, p, i: (b * nq + i, q_col + p)),
                  pl.BlockSpec((seq, LANES), lambda b, p, i: (b, k_col + p)),
                  pl.BlockSpec((seq, LANES), lambda b, p, i: (b, v_col + p)),
                  _resident((tk, tk))],
        out_specs=pl.BlockSpec((tq, LANES), lambda b, p, i: (b * nq + i, p)),
        out_shape=jax.ShapeDtypeStruct((batch * seq, n_groups * LANES), F32),
        scratch_shapes=[pltpu.VMEM((seq, LANES), BF16), pltpu.VMEM((LANES // hd, seq, LANES), BF16)],
        compiler_params=_params(("parallel", "parallel", "arbitrary")),
        name="sb_prefill",
    )(proj, proj, proj, later_sel)


def _cumsum_kernel(x_ref, o_ref, *, chunk):
    rows, t = x_ref.shape
    j = lax.broadcasted_iota(jnp.int32, (chunk, chunk), 0)
    s = lax.broadcasted_iota(jnp.int32, (chunk, chunk), 1)
    incl = (j <= s).astype(F32)
    carry = jnp.zeros((rows, 1), F32)
    for c in range(t // chunk):
        x = x_ref[:, c * chunk:(c + 1) * chunk]
        o_ref[:, c * chunk:(c + 1) * chunk] = carry + jnp.dot(
            x, incl, preferred_element_type=F32, precision=lax.Precision.HIGHEST)
        carry = carry + jnp.sum(x, axis=-1, keepdims=True)


def _cumsum_lanes(x):
    b, r, t = x.shape
    chunk = min(t, MXU_DIM)
    assert t % chunk == 0
    return pl.pallas_call(
        functools.partial(_cumsum_kernel, chunk=chunk),
        grid=(b,),
        in_specs=[pl.BlockSpec((None, r, t), lambda i: (i, 0, 0))],
        out_specs=pl.BlockSpec((None, r, t), lambda i: (i, 0, 0)),
        out_shape=jax.ShapeDtypeStruct((b, r, t), F32),
        compiler_params=_params(("parallel",)),
        name="cumsum",
    )(x)


def _fox_prefill_kernel(q_ref, k_ref, v_ref, cq_ref, ck_ref, o_ref, kb_s, vh_s, *, tq, tk, hd, scale):
    i = pl.program_id(2)
    heads = LANES // hd
    per_tile = tq // tk
    seq = k_ref.shape[0]
    lane = lax.broadcasted_iota(jnp.int32, (1, LANES), 1)
    ones_lane = [((h + 1) % heads) * hd for h in range(heads)]

    @pl.when(i == 0)
    def _():
        def convert(c, carry):
            rows = pl.ds(pl.multiple_of(c * tk, tk), tk)
            kb_s[rows, :] = k_ref[rows, :].astype(BF16)
            v = v_ref[rows, :]
            for h in range(heads):
                vh = jnp.where(_head_lane_mask(hd, h), v, jnp.where(lane == ones_lane[h], 1.0, 0.0))
                vh_s[h, rows, :] = vh.astype(BF16)
            return carry
        lax.fori_loop(0, seq // tk, convert, 0)

    q = q_ref[...] * scale
    qh = [jnp.where(_head_lane_mask(hd, h), q, 0.0).astype(BF16) for h in range(heads)]
    cq = [cq_ref[:, h:h + 1] for h in range(heads)]

    def block(first, width, r0, ms, accs, diagonal):
        n = tq - r0
        nk = width * tk
        keys = pl.ds(pl.multiple_of(first * tk, nk), nk)
        kb = kb_s[keys, :]
        if diagonal:
            causal = (lax.broadcasted_iota(jnp.int32, (n, nk), 1)
                      <= lax.broadcasted_iota(jnp.int32, (n, nk), 0))
        scores = [lax.dot_general(qh[h][r0:], kb, NT_DIMS, preferred_element_type=F32)
                  + cq[h][r0:] - ck_ref[h:h + 1, keys] for h in range(heads)]
        if diagonal:
            scores = [jnp.where(causal, s, -jnp.inf) for s in scores]
        new_ms, new_accs = [], []
        for h in range(heads):
            s = scores[h]
            m_old = ms[h][r0:]
            m_new = jnp.maximum(m_old, jnp.max(s, axis=-1, keepdims=True))
            p = jnp.exp(s - m_new)
            acc = (jnp.exp(m_old - m_new) * accs[h][r0:]
                   + jnp.dot(p.astype(BF16), vh_s[h, keys, :], preferred_element_type=F32))
            if r0:
                m_new = jnp.concatenate([ms[h][:r0], m_new], axis=0)
                acc = jnp.concatenate([accs[h][:r0], acc], axis=0)
            new_ms.append(m_new)
            new_accs.append(acc)
        return tuple(new_ms), tuple(new_accs)

    ms = tuple(jnp.full((tq, 1), MASKED_MAX, F32) for _ in range(heads))
    accs = tuple(jnp.zeros((tq, LANES), F32) for _ in range(heads))
    for d in reversed(range(per_tile)):
        ms, accs = block(i * per_tile + d, 1, d * tk, ms, accs, True)

    past = i * per_tile
    width, done = FOX_WIDE, 0
    while width >= 1:
        count = (past - done) // width if width == FOX_WIDE else ((past - done) // width) % 2
        base = done

        def span(t, state, width=width, base=base):
            return block(base + t * width, width, 0, state[0], state[1], False)

        ms, accs = lax.fori_loop(0, count, span, (ms, accs))
        done = done + count * width
        width //= 2
    out = None
    for h in range(heads):
        denom = accs[h][:, ones_lane[h]:ones_lane[h] + 1]
        part = jnp.where(_head_lane_mask(hd, h), accs[h] / denom, 0.0)
        out = part if out is None else out + part
    o_ref[...] = out


def _fox_prefill(proj, c, batch, seq, q_col, k_col, v_col, n_groups, hd):
    tq = min(seq, FOX_TQ)
    tk = min(seq, FOX_TK)
    nq = seq // tq
    heads = LANES // hd
    assert seq % tq == 0 and tq % tk == 0 and heads >= 2
    ck = c.reshape(batch, n_groups, heads, seq)
    cq = jnp.swapaxes(ck, 2, 3)
    return pl.pallas_call(
        functools.partial(_fox_prefill_kernel, tq=tq, tk=tk, hd=hd, scale=hd ** -0.5),
        grid=(batch, n_groups, nq),
        in_specs=[pl.BlockSpec((tq, LANES), lambda b, p, i: (b * nq + i, q_col + p)),
                  pl.BlockSpec((seq, LANES), lambda b, p, i: (b, k_col + p)),
                  pl.BlockSpec((seq, LANES), lambda b, p, i: (b, v_col + p)),
                  pl.BlockSpec((None, None, tq, heads), lambda b, p, i: (b, p, i, 0)),
                  pl.BlockSpec((None, None, heads, seq), lambda b, p, i: (b, p, 0, 0))],
        out_specs=pl.BlockSpec((tq, LANES), lambda b, p, i: (b * nq + i, p)),
        out_shape=jax.ShapeDtypeStruct((batch * seq, n_groups * LANES), F32),
        scratch_shapes=[pltpu.VMEM((seq, LANES), BF16), pltpu.VMEM((heads, seq, LANES), BF16)],
        compiler_params=_params(("parallel", "parallel", "arbitrary")),
        name="fox_prefill",
    )(proj, proj, proj, cq, ck)


def _mlstm_chunk_kernel(q_ref, k_ref, v_ref, og_ref, irow_ref, frow_ref, ng_ref,
                        c0_ref, n0_ref, m0_ref, gated_ref, c_ref, n_ref, m_ref, *, chunk, n_heads, dk, dv):
    step = pl.program_id(1)

    @pl.when(step == 0)
    def _():
        c_ref[...] = c0_ref[...]
        n_ref[...] = n0_ref[...]
        m_ref[...] = m0_ref[...]

    t_idx = lax.broadcasted_iota(jnp.int32, (chunk, chunk), 0)
    s_idx = lax.broadcasted_iota(jnp.int32, (chunk, chunk), 1)
    causal = s_idx <= t_idx
    diagonal = s_idx == t_idx
    staged = []
    for h in range(n_heads):
        q = q_ref[:, h * dk:(h + 1) * dk]
        k = k_ref[:, h * dk:(h + 1) * dk] * dk ** -0.5
        v = v_ref[:, h * dv:(h + 1) * dv]
        qb, kb = q.astype(BF16), k.astype(BF16)
        b_col = jnp.sum(jnp.where(causal, frow_ref[h], 0.0), axis=1, keepdims=True)
        b_row = jnp.sum(jnp.where(diagonal, b_col, 0.0), axis=0, keepdims=True)
        i_col = jnp.sum(jnp.where(diagonal, irow_ref[h], 0.0), axis=1, keepdims=True)
        m_prev = m_ref[h]
        d = jnp.where(causal, b_col - b_row + irow_ref[h], -jnp.inf)
        g = b_col + m_prev
        m_t = jnp.maximum(g, jnp.max(d, axis=1, keepdims=True))
        wt = jnp.exp(d - m_t) * lax.dot_general(qb, kb, NT_DIMS, preferred_element_type=F32)
        decay = jnp.exp(g - m_t)
        staged.append((q, k, v, qb, kb, b_col, i_col, m_prev, m_t, wt, decay))

    for h in range(n_heads):
        q, k, v, qb, kb, b_col, i_col, m_prev, m_t, wt, decay = staged[h]
        num = (decay * lax.dot_general(qb, c_ref[h].astype(BF16), NT_DIMS, preferred_element_type=F32)
               + jnp.dot(wt.astype(BF16), v.astype(BF16), preferred_element_type=F32))
        den = decay * jnp.sum(q * n_ref[h], axis=1, keepdims=True) + jnp.sum(wt, axis=1, keepdims=True)
        hid = num / jnp.maximum(jnp.abs(den), jnp.exp(-m_t))
        hn = _rms(hid, ng_ref[h])
        gated_ref[:, h * dv:(h + 1) * dv] = (_sigmoid(og_ref[:, h * dv:(h + 1) * dv]) * hn).astype(gated_ref.dtype)

    for h in range(n_heads):
        q, k, v, qb, kb, b_col, i_col, m_prev, m_t, wt, decay = staged[h]
        n_prev = n_ref[h]
        c_prev = c_ref[h]
        b_last = b_col[chunk - 1:chunk, :]
        a_col = b_last - b_col + i_col
        m_new = jnp.maximum(b_last + m_prev, jnp.max(a_col, axis=0, keepdims=True))
        w_col = jnp.exp(a_col - m_new)
        carry_decay = jnp.exp(b_last + m_prev - m_new)
        wv_t = (w_col * v).T.astype(BF16)
        c_ref[h] = carry_decay * c_prev + jnp.dot(wv_t, kb, preferred_element_type=F32)
        n_ref[h] = carry_decay * n_prev + jnp.sum(w_col * k, axis=0, keepdims=True)
        m_ref[h] = m_new


def _mlstm_prefill(proj, gates_i, gates_f, norm_g, c0, n0, m0, batch, seq, n_heads, dk, dv):
    chunk = min(seq, MLSTM_CHUNK)
    nc = seq // chunk
    assert seq % chunk == 0 and dk == dv
    row4 = lambda a: a.reshape(batch, n_heads, 1, seq)
    tok = lambda group: pl.BlockSpec((chunk, n_heads * dk), lambda b, c: (b * nc + c, group))
    rowspec = pl.BlockSpec((None, n_heads, 1, chunk), lambda b, c: (b, 0, 0, c))
    cspec = pl.BlockSpec((None, n_heads, dv, dk), lambda b, c: (b, 0, 0, 0))
    nspec = pl.BlockSpec((None, n_heads, 1, dk), lambda b, c: (b, 0, 0, 0))
    mspec = pl.BlockSpec((None, n_heads, 1, 1), lambda b, c: (b, 0, 0, 0))
    return pl.pallas_call(
        functools.partial(_mlstm_chunk_kernel, chunk=chunk, n_heads=n_heads, dk=dk, dv=dv),
        grid=(batch, nc),
        in_specs=[tok(0), tok(1), tok(2), tok(3), rowspec, rowspec,
                  _resident((n_heads, 1, dv)), cspec, nspec, mspec],
        out_specs=[pl.BlockSpec((chunk, n_heads * dv), lambda b, c: (b * nc + c, 0)), cspec, nspec, mspec],
        out_shape=[jax.ShapeDtypeStruct((batch * seq, n_heads * dv), BF16),
                   jax.ShapeDtypeStruct((batch, n_heads, dv, dk), F32),
                   jax.ShapeDtypeStruct((batch, n_heads, 1, dk), F32),
                   jax.ShapeDtypeStruct((batch, n_heads, 1, 1), F32)],
        compiler_params=_params(("parallel", "arbitrary")),
        name="mlstm_chunk",
    )(proj, proj, proj, proj, row4(gates_i), row4(gates_f),
      norm_g.reshape(n_heads, 1, dv), c0, n0.reshape(batch, n_heads, 1, dk), m0.reshape(batch, n_heads, 1, 1))


def _block_diag_mask(n_heads, hd):
    r = lax.broadcasted_iota(jnp.int32, (n_heads, n_heads * hd), 0)
    lane = lax.broadcasted_iota(jnp.int32, (n_heads, n_heads * hd), 1)
    return (lane >= r * hd) & (lane < (r + 1) * hd)


def _lanes_to_col(row, n):
    r = lax.broadcasted_iota(jnp.int32, (n, row.shape[1]), 0)
    lane = lax.broadcasted_iota(jnp.int32, (n, row.shape[1]), 1)
    return jnp.sum(jnp.where(r == lane, row, 0.0), axis=1, keepdims=True)


def _decode_attn_kernel(pt_ref, row_ref, *refs, group, n_sb, hd_sb, n_fx, hd_fx, cols):
    del pt_ref
    ksb_refs, vsb_refs, kfx_refs, vfx_refs, lft_refs = (refs[g * group:(g + 1) * group] for g in range(5))
    o_ref, qsb_s, qfx_s, carry_sb_s, acc_sb_s, carry_fx_s, m_s, l_s, acc_fx_s = refs[5 * group:]
    p = pl.program_id(1)
    page = ksb_refs[0].shape[1]
    w_sb, w_fx = n_sb * hd_sb, n_fx * hd_fx
    q_sb_col, q_fx_col, k_fx_col, v_fx_col, gate_col = cols
    mask_sb = _block_diag_mask(n_sb, hd_sb)
    mask_fx = _block_diag_mask(n_fx, hd_fx)

    @pl.when(p == 0)
    def _():
        row = row_ref[...]
        q_sb = jnp.where(mask_sb, row[:, q_sb_col:q_sb_col + w_sb] * hd_sb ** -0.5, 0.0)
        q_fx = jnp.where(mask_fx, row[:, q_fx_col:q_fx_col + w_fx] * hd_fx ** -0.5, 0.0)
        qsb_s[...] = q_sb.astype(BF16)
        qfx_s[...] = q_fx.astype(BF16)
        carry_sb_s[...] = jnp.zeros_like(carry_sb_s)
        acc_sb_s[...] = jnp.zeros_like(acc_sb_s)
        k_cur = row[:, k_fx_col:k_fx_col + w_fx]
        v_cur = row[:, v_fx_col:v_fx_col + w_fx]
        m_s[...] = jnp.sum(q_fx * k_cur, axis=1, keepdims=True)
        l_s[...] = jnp.ones_like(l_s)
        acc_fx_s[...] = jnp.broadcast_to(v_cur, acc_fx_s.shape)
        carry_fx_s[...] = _lanes_to_col(row[:, gate_col:gate_col + LANES], n_fx)

    j = lax.broadcasted_iota(jnp.int32, (page, page), 0)
    s = lax.broadcasted_iota(jnp.int32, (page, page), 1)
    later_sel = (j > s).astype(F32)

    def lanes(refs, dtype):
        return jnp.concatenate([r[...].astype(dtype) for r in refs], axis=1)

    def later_bias(x, carry):
        parts = [x[:, g * page:(g + 1) * page] for g in range(group)]
        inside = jnp.dot(jnp.concatenate(parts, axis=0), later_sel, preferred_element_type=F32,
                         precision=lax.Precision.HIGHEST)
        rows = x.shape[0]
        out = []
        for g in range(group):
            out.append(inside[g * rows:(g + 1) * rows] + carry)
            carry = carry + jnp.sum(parts[g], axis=1, keepdims=True)
        return jnp.concatenate(out, axis=1), carry

    z = jnp.dot(qsb_s[...], lanes(ksb_refs, BF16), preferred_element_type=F32)
    log_beta = _log_sigmoid(z)
    later, carry_sb_s[...] = later_bias(log_beta - z, carry_sb_s[...])
    w = jnp.exp(log_beta + later)
    acc_sb_s[...] += lax.dot_general(w.astype(BF16), lanes(vsb_refs, BF16), NT_DIMS, preferred_element_type=F32)

    bias, carry_fx_s[...] = later_bias(lanes(lft_refs, F32), carry_fx_s[...])
    sc = jnp.dot(qfx_s[...], lanes(kfx_refs, BF16), preferred_element_type=F32) + bias
    m_new = jnp.maximum(m_s[...], jnp.max(sc, axis=1, keepdims=True))
    alpha = jnp.exp(m_s[...] - m_new)
    pr = jnp.exp(sc - m_new)
    l_s[...] = alpha * l_s[...] + jnp.sum(pr, axis=1, keepdims=True)
    acc_fx_s[...] = alpha * acc_fx_s[...] + lax.dot_general(
        pr.astype(BF16), lanes(vfx_refs, BF16), NT_DIMS, preferred_element_type=F32)
    m_s[...] = m_new

    @pl.when(p == pl.num_programs(1) - 1)
    def _():
        o_sb = jnp.sum(jnp.where(mask_sb, acc_sb_s[...], 0.0), axis=0, keepdims=True)
        o_fx = jnp.sum(jnp.where(mask_fx, acc_fx_s[...] / l_s[...], 0.0), axis=0, keepdims=True)
        o_ref[:, 0:w_sb] = o_sb
        o_ref[:, w_sb:w_sb + w_fx] = o_fx


def _decode_attn(proj, page_table, layer, cache_sb_k, cache_sb_v, cache_fox_k, cache_fox_v, cache_fox_logf, cols):
    b, n = proj.shape
    _, n_phys, page, n_sb, hd_sb = cache_sb_k.shape
    n_fx, hd_fx = cache_fox_k.shape[-2:]
    n_pages = page_table.shape[1]
    w_sb, w_fx = n_sb * hd_sb, n_fx * hd_fx
    flat = lambda a: jnp.transpose(a, (0, 1, 3, 4, 2)).reshape(a.shape[0], n_phys, -1, page)
    lft = jnp.swapaxes(cache_fox_logf, 2, 3)

    group = min(n_pages, DECODE_PAGES_PER_STEP)
    assert n_pages % group == 0

    def paged(rows, g):
        return pl.BlockSpec((None, None, rows, page),
                            lambda i, p, pt: (layer, pt[i, n_pages - 1 - p * group - g], 0, 0))

    caches = (flat(cache_sb_k), flat(cache_sb_v), flat(cache_fox_k), flat(cache_fox_v), lft)
    rows = (w_sb, w_sb, w_fx, w_fx, n_fx)
    grid_spec = pltpu.PrefetchScalarGridSpec(
        num_scalar_prefetch=1,
        grid=(b, n_pages // group),
        in_specs=[pl.BlockSpec((None, 1, n), lambda i, p, pt: (i, 0, 0))]
        + [paged(r, g) for r in rows for g in range(group)],
        out_specs=pl.BlockSpec((None, 1, w_sb + w_fx), lambda i, p, pt: (i, 0, 0)),
        scratch_shapes=[pltpu.VMEM((n_sb, w_sb), BF16), pltpu.VMEM((n_fx, w_fx), BF16),
                        pltpu.VMEM((n_sb, 1), F32), pltpu.VMEM((n_sb, w_sb), F32),
                        pltpu.VMEM((n_fx, 1), F32), pltpu.VMEM((n_fx, 1), F32), pltpu.VMEM((n_fx, 1), F32),
                        pltpu.VMEM((n_fx, w_fx), F32)])
    out = pl.pallas_call(
        functools.partial(_decode_attn_kernel, group=group, n_sb=n_sb, hd_sb=hd_sb, n_fx=n_fx, hd_fx=hd_fx,
                          cols=cols),
        grid_spec=grid_spec,
        out_shape=jax.ShapeDtypeStruct((b, 1, w_sb + w_fx), F32),
        compiler_params=_params(("parallel", "arbitrary")),
        name="decode_attn",
    )(page_table, proj.reshape(b, 1, n), *[c for c in caches for _ in range(group)])
    return out.reshape(b, w_sb + w_fx)


def _mlstm_step_kernel(row_ref, ng_ref, c0_ref, n0_ref, m0_ref, gated_ref, c_ref, n_ref, m_ref,
                       *, n_heads, dk, dv, gate_col):
    row = row_ref[...]
    gates = row[:, gate_col:gate_col + LANES]
    eye = (lax.broadcasted_iota(jnp.int32, (dv, dv), 0) == lax.broadcasted_iota(jnp.int32, (dv, dv), 1))
    for h in range(n_heads):
        q = row[:, h * dk:(h + 1) * dk]
        k = row[:, (n_heads + h) * dk:(n_heads + h + 1) * dk] * dk ** -0.5
        v = row[:, 2 * n_heads * dk + h * dv:2 * n_heads * dk + (h + 1) * dv]
        og = row[:, 2 * n_heads * dk + (n_heads + h) * dv:2 * n_heads * dk + (n_heads + h + 1) * dv]
        i_log = gates[:, h:h + 1]
        f_log = gates[:, n_heads + h:n_heads + h + 1]
        c_prev = c0_ref[h]
        n_prev = n0_ref[h:h + 1, :]
        m_prev = m0_ref[h:h + 1, :]
        g = f_log + m_prev
        m_t = jnp.maximum(g, i_log)
        w_in = jnp.exp(i_log - m_t)
        decay = jnp.exp(g - m_t)
        qk = jnp.sum(q * k, axis=1, keepdims=True)
        q8 = jnp.broadcast_to(q, (SUBLANES, dk)).astype(BF16)
        cq = lax.dot_general(q8, c_prev.astype(BF16), NT_DIMS, preferred_element_type=F32)[0:1, :]
        num = decay * cq + (w_in * qk) * v
        den = decay * jnp.sum(q * n_prev, axis=1, keepdims=True) + w_in * qk
        hid = num / jnp.maximum(jnp.abs(den), jnp.exp(-m_t))
        hn = _rms(hid, ng_ref[h:h + 1, :])
        gated_ref[:, h * dv:(h + 1) * dv] = _sigmoid(og) * hn
        v_col = jnp.sum(jnp.where(eye, v, 0.0), axis=1, keepdims=True)
        c_ref[h] = decay * c_prev + v_col * (w_in * k)
        n_ref[h:h + 1, :] = decay * n_prev + w_in * k
        m_ref[h:h + 1, :] = m_t


def _mlstm_decode(proj, norm_g, c0, n0, m0, gate_col):
    b, n = proj.shape
    _, n_heads, dv, dk = c0.shape
    seq_spec = lambda *tail: pl.BlockSpec((None,) + tail, lambda i: (i,) + (0,) * len(tail))
    gated, c, nn, m = pl.pallas_call(
        functools.partial(_mlstm_step_kernel, n_heads=n_heads, dk=dk, dv=dv, gate_col=gate_col),
        grid=(b,),
        in_specs=[seq_spec(1, n), _resident((n_heads, dv)), seq_spec(n_heads, dv, dk),
                  seq_spec(n_heads, dk), seq_spec(n_heads, 1)],
        out_specs=[seq_spec(1, n_heads * dv), seq_spec(n_heads, dv, dk), seq_spec(n_heads, dk),
                   seq_spec(n_heads, 1)],
        out_shape=[jax.ShapeDtypeStruct((b, 1, n_heads * dv), F32),
                   jax.ShapeDtypeStruct((b, n_heads, dv, dk), F32),
                   jax.ShapeDtypeStruct((b, n_heads, dk), F32),
                   jax.ShapeDtypeStruct((b, n_heads, 1), F32)],
        compiler_params=_params(("parallel",)),
        name="mlstm_step",
    )(proj.reshape(b, 1, n), norm_g, c0, n0, m0.reshape(b, n_heads, 1))
    return gated.reshape(b, n_heads * dv), c, nn, m.reshape(b, n_heads)


def _pad_gate_cols(w, n_main):
    n_gates = w.shape[1] - n_main
    return jnp.pad(w, ((0, 0), (0, LANES - n_gates))).astype(BF16)


def _gate_rows(bias, logsig_from):
    n = bias.shape[0]
    gb = jnp.pad(bias.astype(F32), (0, LANES - n)).reshape(1, LANES)
    gm = (jnp.arange(LANES) >= logsig_from) & (jnp.arange(LANES) < n)
    return gb, gm.astype(F32).reshape(1, LANES)


def _trunk(x, caches, page_table, ml_state, p):
    bsz, seq, d = x.shape
    m = bsz * seq
    x = x.reshape(m, d)
    depth = p["norm_g"].shape[0]
    n_sb, hd_sb, n_fx, hd_fx = p["n_sb"], p["hd_sb"], p["n_fx"], p["hd_fx"]
    w_sb, w_fx = n_sb * hd_sb, n_fx * hd_fx
    n_ml, dk, dv = p["n_ml"], p["dk"], p["dv"]
    att_rows, ml_rows = [], []
    for layer in range(depth):
        idx = layer // 2
        g = lambda j: p["norm_g"][layer, j].reshape(1, d)
        x = _ffn(x, g(0), p["ffn1_w_in"], p["ffn1_w_out"], g(1), layer)
        if layer % 2 == 0:
            gb, gm = _gate_rows(p["att_b_f"][idx], 0)
            main = 3 * w_sb + 3 * w_fx
            kv_cols = (w_sb, 2 * w_sb, 3 * w_sb + w_fx, 3 * w_sb + 2 * w_fx)
            kv_heads = ((n_sb, hd_sb), (n_sb, hd_sb), (n_fx, hd_fx), (n_fx, hd_fx))
            if caches is None:
                proj, *kv_t = _proj(x, g(2), p["att_w_in"][idx], gb, gm, seq, kv_cols)
                k_sb, v_sb, k_fx, v_fx = (
                    jnp.transpose(t.reshape(bsz, nh, hd, seq), (0, 3, 1, 2)) for t, (nh, hd) in zip(kv_t, kv_heads))
            else:
                proj, = _proj(x, g(2), p["att_w_in"][idx], gb, gm)
                k_sb, v_sb, k_fx, v_fx = (
                    proj[:, c:c + nh * hd].reshape(bsz, seq, nh, hd) for c, (nh, hd) in zip(kv_cols, kv_heads))
            lf = proj[:, main:main + n_fx].reshape(bsz, seq, n_fx)
            att_rows.append((k_sb, v_sb, k_fx, v_fx, lf))
            w_out = p["att_w_out"][idx]
            if caches is None:
                cb = lambda off: off // LANES
                o_sb = _sb_prefill(proj, bsz, seq, cb(0), cb(w_sb), cb(2 * w_sb), w_sb // LANES, hd_sb)
                c = _cumsum_lanes(jnp.swapaxes(lf, 1, 2))
                o_fx = _fox_prefill(proj, c, bsz, seq, cb(3 * w_sb), cb(3 * w_sb + w_fx),
                                    cb(3 * w_sb + 2 * w_fx), w_fx // LANES, hd_fx)
                mixer = ([o_sb, o_fx], [w_out[:w_sb], w_out[w_sb:]], g(3))
            else:
                assert seq == 1
                cols = (0, 3 * w_sb, 3 * w_sb + w_fx, 3 * w_sb + 2 * w_fx, main)
                o = _decode_attn(proj, page_table, idx, *caches, cols)
                mixer = ([o], [w_out], g(3))
        else:
            gb, gm = _gate_rows(jnp.concatenate([p["ml_b_i"][idx], p["ml_b_f"][idx]]), n_ml)
            proj, = _proj(x, g(2), p["ml_w_in"][idx], gb, gm)
            main = 2 * n_ml * dk + 2 * n_ml * dv
            if ml_state is None:
                gates = proj[:, main:main + 2 * n_ml].reshape(bsz, seq, 2 * n_ml)
                gates = jnp.transpose(gates, (0, 2, 1))
                zeros = lambda *s: jnp.zeros(s, F32)
                gated, c_new, n_new, m_new = _mlstm_prefill(
                    proj, gates[:, :n_ml], gates[:, n_ml:], p["ml_norm_g"][idx],
                    zeros(bsz, n_ml, dv, dk), zeros(bsz, n_ml, dk), zeros(bsz, n_ml), bsz, seq, n_ml, dk, dv)
                n_new = n_new.reshape(bsz, n_ml, dk)
                m_new = m_new.reshape(bsz, n_ml)
            else:
                assert seq == 1
                gated, c_new, n_new, m_new = _mlstm_decode(
                    proj, p["ml_norm_g"][idx], ml_state[0][idx], ml_state[1][idx], ml_state[2][idx], main)
            ml_rows.append((c_new, n_new, m_new))
            mixer = ([gated], [p["ml_w_out"][idx]], g(3))
        x = _ffn(x, g(4), p["ffn2_w_in"], p["ffn2_w_out"], g(5), layer, mixer)
    att_new = [jnp.stack([r[i] for r in att_rows]) for i in range(5)]
    ml_new = [jnp.stack([r[i] for r in ml_rows]) for i in range(3)]
    return x.reshape(bsz, seq, d), att_new, ml_new


def kernel(x_prompt, x_sample, cache_sb_k, cache_sb_v, cache_fox_k, cache_fox_v, cache_fox_logf,
           state_mlstm_C, state_mlstm_n, state_mlstm_m, page_table, norm_g, ffn1_w_in, ffn1_w_out,
           ffn2_w_in, ffn2_w_out, att_w_in, att_b_f, att_w_out, ml_w_in, ml_b_i, ml_b_f, ml_norm_g,
           ml_w_out):
    n_sb, hd_sb = cache_sb_k.shape[-2:]
    n_fx, hd_fx = cache_fox_k.shape[-2:]
    n_ml, dv, dk = state_mlstm_C.shape[-3:]
    att_main = 3 * n_sb * hd_sb + 3 * n_fx * hd_fx
    ml_main = 2 * n_ml * dk + 2 * n_ml * dv
    p = dict(
        norm_g=norm_g, n_sb=n_sb, hd_sb=hd_sb, n_fx=n_fx, hd_fx=hd_fx, n_ml=n_ml, dk=dk, dv=dv,
        ffn1_w_in=ffn1_w_in.astype(BF16), ffn1_w_out=ffn1_w_out.astype(BF16),
        ffn2_w_in=ffn2_w_in.astype(BF16), ffn2_w_out=ffn2_w_out.astype(BF16),
        att_w_in=jnp.stack([_pad_gate_cols(w, att_main) for w in att_w_in]), att_b_f=att_b_f,
        att_w_out=att_w_out.astype(BF16),
        ml_w_in=jnp.stack([_pad_gate_cols(w, ml_main) for w in ml_w_in]), ml_b_i=ml_b_i, ml_b_f=ml_b_f,
        ml_norm_g=ml_norm_g, ml_w_out=ml_w_out.astype(BF16))
    y_p, att_p, ml_p = _trunk(x_prompt, None, None, None, p)
    caches = (cache_sb_k, cache_sb_v, cache_fox_k, cache_fox_v, cache_fox_logf)
    y_s, att_s, ml_s = _trunk(x_sample, caches, page_table,
                              (state_mlstm_C, state_mlstm_n, state_mlstm_m), p)
    return (y_p, y_s, *att_p, *ml_p, *att_s, *ml_s)
```
